```python
import jax
import jax.numpy as jnp
from jax import lax
import numpy as np

D_MODEL = 1024
BATCH = 8
SEQ = 2048
DEPTH = 4
DEC_BATCH = 128
DEC_SEQ = 8
PAST_LEN = 2048
PAGE_SIZE = 128

HEAD_DIM = 64
A_HEADS = 6
A_WIDTH = A_HEADS * HEAD_DIM
DILATED = ((128, 1), (512, 4), (2048, 16))
MAX_WINDOW = 2048
BAND_BLOCK = 128
LRU_BLOCKS = 6
LRU_BLOCK_DIM = 64
LRU_WIDTH = LRU_BLOCKS * LRU_BLOCK_DIM
CONV_W = 4
LRU_C = 8.0
C_HEADS = 6
C_KDIM = 128
C_VDIM = 64
C_KWIDTH = C_HEADS * C_KDIM
C_VWIDTH = C_HEADS * C_VDIM
C_CHUNK = 64
MIX_WIDTH = A_WIDTH + LRU_WIDTH + C_VWIDTH
SPLIT_SIZES = (A_WIDTH, A_WIDTH, A_WIDTH, A_WIDTH, LRU_WIDTH, LRU_WIDTH, C_KWIDTH, C_KWIDTH, C_VWIDTH, C_VWIDTH)
IN_WIDTH = sum(SPLIT_SIZES)
RMS_EPS = 1e-6

kernel_name = 'hybrid_dilated_rglru_hgrn2_step'


def rmsnorm(x, g):
    xf = x.astype(jnp.float32)
    y = xf * lax.rsqrt(jnp.mean(xf * xf, axis=-1, keepdims=True) + RMS_EPS)
    return (y * g.astype(jnp.float32)).astype(x.dtype)


def split_columns(z):
    parts, start = [], 0
    for size in SPLIT_SIZES:
        parts.append(z[..., start:start + size])
        start += size
    return parts


def dilated_branch_prompt(q, k, v, window, dilation):
    b, s, h, dh = q.shape
    n_back = window // dilation
    m = s // dilation
    nb = -(-m // BAND_BLOCK)
    mp = nb * BAND_BLOCK

    def to_blocks(t):
        t = t.reshape(b, m, dilation, h, dh).transpose(0, 2, 1, 3, 4)
        t = jnp.pad(t, ((0, 0), (0, 0), (0, mp - m), (0, 0), (0, 0)))
        return t.reshape(b, dilation, nb, BAND_BLOCK, h, dh)

    def band(t):
        prev = jnp.pad(t, ((0, 0), (0, 0), (1, 0), (0, 0), (0, 0), (0, 0)))[:, :, :-1]
        return jnp.concatenate([prev, t], axis=3)

    qb = to_blocks(q)
    kband = band(to_blocks(k))
    vband = band(to_blocks(v))
    scores = jnp.einsum('brnqhd,brnkhd->brnhqk', qb, kband,
                        preferred_element_type=jnp.float32) * (dh ** -0.5)
    qi = jnp.arange(BAND_BLOCK)[:, None]
    ki = jnp.arange(2 * BAND_BLOCK)[None, :]
    dist = BAND_BLOCK + qi - ki
    blk = jnp.arange(nb)[:, None, None]
    valid = (dist >= 0) & (dist <= n_back) & ((blk > 0) | (ki >= BAND_BLOCK))
    scores = jnp.where(valid[:, None], scores, -jnp.inf)
    mx = jnp.max(scores, axis=-1, keepdims=True)
    p = jnp.exp(scores - mx)
    l = jnp.sum(p, axis=-1, keepdims=True)
    o = jnp.einsum('brnhqk,brnkhd->brnqhd', p, vband.astype(jnp.float32)) / jnp.swapaxes(l, 3, 4)
    lse = jnp.swapaxes((mx + jnp.log(l))[..., 0], 3, 4)

    def from_blocks(t):
        t = t.reshape(b, dilation, mp, *t.shape[4:])[:, :, :m]
        t = jnp.moveaxis(t, 1, 2)
        return t.reshape(b, s, *t.shape[3:])

    return from_blocks(o), from_blocks(lse)


def dilated_branch_sample(q, k_all, v_all, window, dilation, n_buf):
    t_len, dh = q.shape[1], q.shape[-1]
    n_back = window // dilation
    idx = n_buf + jnp.arange(t_len)[:, None] - dilation * jnp.arange(n_back + 1)[None, :]
    valid = idx >= 0
    idx = jnp.maximum(idx, 0)
    kg = jnp.take(k_all, idx, axis=1)
    vg = jnp.take(v_all, idx, axis=1)
    scores = jnp.einsum('bthd,btnhd->bthn', q, kg, preferred_element_type=jnp.float32) * (dh ** -0.5)
    scores = jnp.where(valid[:, None, :], scores, -jnp.inf)
    mx = jnp.max(scores, axis=-1, keepdims=True)
    p = jnp.exp(scores - mx)
    l = jnp.sum(p, axis=-1, keepdims=True)
    o = jnp.einsum('bthn,btnhd->bthd', p, vg.astype(jnp.float32)) / l
    return o, (mx + jnp.log(l))[..., 0]


def combine_by_denominator(branches):
    outs = jnp.stack([o for o, _ in branches], axis=0)
    lses = jnp.stack([l for _, l in branches], axis=0)
    wts = jax.nn.softmax(lses, axis=0)
    return jnp.sum(wts[..., None] * outs, axis=0)


def causal_conv(x_pad, conv_w, conv_b):
    s = x_pad.shape[1] - (CONV_W - 1)
    out = conv_b
    for j in range(CONV_W):
        out = out + x_pad[:, j:j + s] * conv_w[j]
    return out


def rg_lru(xc, h0, w_a, b_a, w_x, b_x, lam):
    bsz, s, _ = xc.shape
    xb = xc.reshape(bsz, s, LRU_BLOCKS, LRU_BLOCK_DIM)
    r = jax.nn.sigmoid(jnp.einsum('bshi,hij->bshj', xb, w_a).reshape(bsz, s, LRU_WIDTH).astype(jnp.float32)
                       + b_a.astype(jnp.float32))
    i = jax.nn.sigmoid(jnp.einsum('bshi,hij->bshj', xb, w_x).reshape(bsz, s, LRU_WIDTH).astype(jnp.float32)
                       + b_x.astype(jnp.float32))
    log_a = -LRU_C * r * jax.nn.softplus(-lam.astype(jnp.float32))
    a = jnp.exp(log_a)
    u = jnp.sqrt(-jnp.expm1(2.0 * log_a)) * (i * xc.astype(jnp.float32))

    def combine(e1, e2):
        a1, b1 = e1
        a2, b2 = e2
        return a1 * a2, a2 * b1 + b2

    a_cum, u_cum = lax.associative_scan(combine, (a, u), axis=1)
    h = a_cum * h0.astype(jnp.float32)[:, None] + u_cum
    return h, h[:, -1]


def hgrn2(q, k, v, logf, s0):
    bsz, s = q.shape[:2]
    L = min(C_CHUNK, s)
    nc = -(-s // L)
    pad = nc * L - s

    def chunks(t):
        t = jnp.pad(t, ((0, 0), (0, pad), (0, 0), (0, 0)))
        return t.reshape(bsz, nc, L, *t.shape[2:]).transpose(1, 0, 3, 2, 4)

    qc, kc, vc, fc = chunks(q), chunks(k), chunks(v), chunks(logf)
    causal = jnp.tril(jnp.ones((L, L), dtype=bool))

    def step(state, inp):
        qt, kt, vt, ft = inp
        bcum = jnp.cumsum(ft, axis=2)
        o_inter = jnp.einsum('bhld,bhdv->bhlv', qt * jnp.exp(bcum), state)
        diff = bcum[:, :, :, None, :] - bcum[:, :, None, :, :]
        decay = jnp.exp(jnp.where(causal[:, :, None], diff, -jnp.inf))
        attn = jnp.einsum('bhtd,bhtsd,bhsd->bhts', qt, decay, kt)
        o = o_inter + jnp.einsum('bhts,bhsv->bhtv', attn, vt)
        blast = bcum[:, :, -1:]
        new_state = (jnp.exp(blast[:, :, 0])[..., None] * state
                     + jnp.einsum('bhsd,bhsv->bhdv', kt * jnp.exp(blast - bcum), vt))
        return new_state, o

    s_fin, o = lax.scan(step, s0.astype(jnp.float32), (qc, kc, vc, fc))
    o = o.transpose(1, 0, 3, 2, 4).reshape(bsz, nc * L, C_HEADS, C_VDIM)[:, :s]
    return o, s_fin


def hybrid_layer(x, kv_buf, conv_buf, h0, s0, lb, norm_g, w_in, conv_w, conv_b,
                 w_a, b_a, w_x, b_x, lam, c_norm_g, w_out):
    bsz, s, _ = x.shape
    h = rmsnorm(x, norm_g)
    z = h @ w_in
    qa, ka, va, ga, xb, gb, qc, fc, ic, gc = split_columns(z)

    qa = qa.reshape(bsz, s, A_HEADS, HEAD_DIM)
    ka = ka.reshape(bsz, s, A_HEADS, HEAD_DIM)
    va = va.reshape(bsz, s, A_HEADS, HEAD_DIM)
    if kv_buf is None:
        branches = [dilated_branch_prompt(qa, ka, va, w, r) for (w, r) in DILATED]
        keep = min(MAX_WINDOW, s)
        k_state, v_state = ka[:, s - keep:], va[:, s - keep:]
    else:
        k_buf, v_buf = kv_buf
        n_buf = k_buf.shape[1]
        k_all = jnp.concatenate([k_buf.astype(ka.dtype), ka], axis=1)
        v_all = jnp.concatenate([v_buf.astype(va.dtype), va], axis=1)
        branches = [dilated_branch_sample(qa, k_all, v_all, w, r, n_buf) for (w, r) in DILATED]
        k_state, v_state = ka, va
    o_a = combine_by_denominator(branches).reshape(bsz, s, A_WIDTH) * jax.nn.silu(ga.astype(jnp.float32))

    x_pad = jnp.concatenate([conv_buf.astype(xb.dtype), xb], axis=1)
    xc = causal_conv(x_pad, conv_w, conv_b)
    hb, h_last = rg_lru(xc, h0, w_a, b_a, w_x, b_x, lam)
    o_b = hb * jax.nn.silu(gb.astype(jnp.float32))
    conv_state = x_pad[:, -(CONV_W - 1):]

    q = jax.nn.silu(qc.astype(jnp.float32)).reshape(bsz, s, C_HEADS, C_KDIM) * (C_KDIM ** -0.5)
    lbh = lb.reshape(C_HEADS, C_KDIM)
    g = lbh + (1.0 - lbh) * jax.nn.sigmoid(fc.astype(jnp.float32)).reshape(bsz, s, C_HEADS, C_KDIM)
    v = ic.astype(jnp.float32).reshape(bsz, s, C_HEADS, C_VDIM)
    oc, s_fin = hgrn2(q, 1.0 - g, v, jnp.log(g), s0)
    oc = oc * lax.rsqrt(jnp.mean(oc * oc, axis=-1, keepdims=True) + RMS_EPS)
    oc = (oc * c_norm_g.astype(jnp.float32).reshape(C_HEADS, C_VDIM)).reshape(bsz, s, C_VWIDTH)
    o_c = oc * jax.nn.silu(gc.astype(jnp.float32))

    mixed = jnp.concatenate([o_a, o_b, o_c], axis=-1).astype(x.dtype)
    y = x + mixed @ w_out
    return y, (k_state, v_state, conv_state, h_last, s_fin)


def setup_inputs(seed: int = 0) -> dict:
    key = jax.random.key(seed)
    ks = jax.random.split(key, 20)
    f32 = jnp.float32
    w_buf = min(MAX_WINDOW, PAST_LEN)
    nrm = lambda k, shape: jax.random.normal(k, shape, dtype=f32)
    return {
        'x_prompt': nrm(ks[0], (BATCH, SEQ, D_MODEL)),
        'x_sample': nrm(ks[1], (DEC_BATCH, DEC_SEQ, D_MODEL)),
        'cache_k_win': nrm(ks[2], (DEPTH, DEC_BATCH, w_buf, A_HEADS, HEAD_DIM)),
        'cache_v_win': nrm(ks[3], (DEPTH, DEC_BATCH, w_buf, A_HEADS, HEAD_DIM)),
        'state_conv': 0.5 * nrm(ks[4], (DEPTH, DEC_BATCH, CONV_W - 1, LRU_WIDTH)),
        'state_lru': 0.5 * nrm(ks[5], (DEPTH, DEC_BATCH, LRU_WIDTH)),
        'state_hgrn': 0.3 * nrm(ks[6], (DEPTH, DEC_BATCH, C_HEADS, C_KDIM, C_VDIM)),
        'norm_g': 1.0 + 0.02 * nrm(ks[7], (DEPTH, D_MODEL)),
        'w_in': nrm(ks[8], (DEPTH, D_MODEL, IN_WIDTH)) * D_MODEL ** -0.5,
        'conv_w': nrm(ks[9], (DEPTH, CONV_W, LRU_WIDTH)) * CONV_W ** -0.5,
        'conv_b': 0.01 * nrm(ks[10], (DEPTH, LRU_WIDTH)),
        'lru_w_a': nrm(ks[11], (DEPTH, LRU_BLOCKS, LRU_BLOCK_DIM, LRU_BLOCK_DIM)) * LRU_BLOCK_DIM ** -0.5,
        'lru_b_a': 0.01 * nrm(ks[12], (DEPTH, LRU_WIDTH)),
        'lru_w_x': nrm(ks[13], (DEPTH, LRU_BLOCKS, LRU_BLOCK_DIM, LRU_BLOCK_DIM)) * LRU_BLOCK_DIM ** -0.5,
        'lru_b_x': 0.01 * nrm(ks[14], (DEPTH, LRU_WIDTH)),
        'lru_lambda': jax.random.uniform(ks[15], (DEPTH, LRU_WIDTH), dtype=f32, minval=4.3, maxval=9.0),
        'hgrn_lb_logits': 0.1 * nrm(ks[16], (DEPTH, C_KWIDTH)),
        'hgrn_norm_g': 1.0 + 0.02 * nrm(ks[17], (DEPTH, C_VWIDTH)),
        'w_out': nrm(ks[18], (DEPTH, MIX_WIDTH, D_MODEL)) * MIX_WIDTH ** -0.5,
        'final_norm_g': 1.0 + 0.02 * nrm(ks[19], (D_MODEL,)),
    }


def reference(x_prompt, x_sample, cache_k_win, cache_v_win, state_conv, state_lru, state_hgrn,
              norm_g, w_in, conv_w, conv_b, lru_w_a, lru_b_a, lru_w_x, lru_b_x, lru_lambda,
              hgrn_lb_logits, hgrn_norm_g, w_out, final_norm_g):
    lb_w = jax.nn.softmax(hgrn_lb_logits.astype(jnp.float32), axis=0)
    lb_all = jnp.cumsum(lb_w, axis=0) - lb_w[0]
    bp = x_prompt.shape[0]
    xp, xs = x_prompt, x_sample
    sp, ss = [], []
    for l in range(DEPTH):
        lw = (norm_g[l], w_in[l], conv_w[l], conv_b[l], lru_w_a[l], lru_b_a[l], lru_w_x[l], lru_b_x[l],
              lru_lambda[l], hgrn_norm_g[l], w_out[l])
        xp, st_p = hybrid_layer(
            xp, None,
            jnp.zeros((bp, CONV_W - 1, LRU_WIDTH), dtype=xp.dtype),
            jnp.zeros((bp, LRU_WIDTH), dtype=jnp.float32),
            jnp.zeros((bp, C_HEADS, C_KDIM, C_VDIM), dtype=jnp.float32),
            lb_all[l], *lw)
        xs, st_s = hybrid_layer(
            xs, (cache_k_win[l], cache_v_win[l]), state_conv[l], state_lru[l], state_hgrn[l],
            lb_all[l], *lw)
        sp.append(st_p)
        ss.append(st_s)
    y_prompt = rmsnorm(xp, final_norm_g)
    y_sample = rmsnorm(xs, final_norm_g)
    k_win_prompt = jnp.stack([t[0] for t in sp])
    v_win_prompt = jnp.stack([t[1] for t in sp])
    k_win_sample = jnp.stack([t[0] for t in ss])
    v_win_sample = jnp.stack([t[1] for t in ss])
    conv_prompt = jnp.stack([t[2] for t in sp])
    conv_sample = jnp.stack([t[2] for t in ss])
    lru_prompt = jnp.stack([t[3] for t in sp])
    lru_sample = jnp.stack([t[3] for t in ss])
    hgrn_prompt = jnp.stack([t[4] for t in sp])
    hgrn_sample = jnp.stack([t[4] for t in ss])
    return (y_prompt, y_sample, k_win_prompt, v_win_prompt, k_win_sample, v_win_sample,
            conv_prompt, conv_sample, lru_prompt, lru_sample, hgrn_prompt, hgrn_sample)
```

```python
import functools

import jax
import jax.numpy as jnp
import numpy as np
from jax import lax
from jax.experimental import pallas as pl
from jax.experimental.pallas import tpu as pltpu

F32 = jnp.float32
BF16 = jnp.bfloat16

D_MODEL = 1024
DEPTH = 4
HEAD_DIM = 64
A_HEADS = 6
A_WIDTH = A_HEADS * HEAD_DIM
DILATED = ((128, 1), (512, 4), (2048, 16))
MAX_WINDOW = 2048
LRU_BLOCKS = 6
LRU_BLOCK_DIM = 64
LRU_WIDTH = LRU_BLOCKS * LRU_BLOCK_DIM
CONV_W = 4
LRU_C = 8.0
C_HEADS = 6
C_KDIM = 128
C_VDIM = 64
C_KWIDTH = C_HEADS * C_KDIM
C_VWIDTH = C_HEADS * C_VDIM
MIX_WIDTH = A_WIDTH + LRU_WIDTH + C_VWIDTH
IN_WIDTH = 4 * A_WIDTH + 2 * LRU_WIDTH + 2 * C_KWIDTH + 2 * C_VWIDTH
RMS_EPS = 1e-6

LANES = 128
SUBLANES = 8
HEAD_PAIRS = A_HEADS // 2
VMEM_LIMIT = 56 * 1024 * 1024

COL_QA, COL_KA, COL_VA, COL_GA = 0, A_WIDTH, 2 * A_WIDTH, 3 * A_WIDTH
COL_XB = 4 * A_WIDTH
COL_GB = COL_XB + LRU_WIDTH
COL_QC = COL_GB + LRU_WIDTH
COL_FC = COL_QC + C_KWIDTH
COL_IC = COL_FC + C_KWIDTH
COL_GC = COL_IC + C_VWIDTH

ATT_TQ = 256
ATT_TK = 256
HGRN_CHUNK = 64
HGRN_TILE = 512
LRU_TILE = 512
PROJ_TM = 256


def _silu(x):
    return x * jax.nn.sigmoid(x)


def _params(sem, **kw):
    return pltpu.CompilerParams(dimension_semantics=sem, vmem_limit_bytes=VMEM_LIMIT, **kw)


def _inproj_kernel(x_ref, g_ref, w_ref, z_ref, *, n_chunk):
    x = x_ref[...]
    h = (x * lax.rsqrt(jnp.mean(x * x, axis=-1, keepdims=True) + RMS_EPS) * g_ref[...]).astype(BF16)
    for n in range(IN_WIDTH // n_chunk):
        sl = slice(n * n_chunk, (n + 1) * n_chunk)
        z_ref[:, sl] = jnp.dot(h, w_ref[:, sl], preferred_element_type=F32)


def _inproj(x2d, norm_g3, w_in_bf, layer):
    m = x2d.shape[0]
    tm = min(PROJ_TM, m)
    assert m % tm == 0
    return pl.pallas_call(
        functools.partial(_inproj_kernel, n_chunk=512),
        grid=(m // tm,),
        in_specs=[
            pl.BlockSpec((tm, D_MODEL), lambda i: (i, 0)),
            pl.BlockSpec((None, 1, D_MODEL), lambda i: (layer, 0, 0)),
            pl.BlockSpec((None, D_MODEL, IN_WIDTH), lambda i: (layer, 0, 0)),
        ],
        out_specs=pl.BlockSpec((tm, IN_WIDTH), lambda i: (i, 0)),
        out_shape=jax.ShapeDtypeStruct((m, IN_WIDTH), F32),
        compiler_params=_params(("parallel",)),
        name="inproj",
    )(x2d, norm_g3, w_in_bf)


def _outproj_kernel(x_ref, ma_ref, mb_ref, mc_ref, w_ref, g_ref, y_ref, *, final_norm):
    y = x_ref[...]
    y = y + jnp.dot(ma_ref[...].astype(BF16), w_ref[0:A_WIDTH, :], preferred_element_type=F32)
    y = y + jnp.dot(mb_ref[...].astype(BF16), w_ref[A_WIDTH:A_WIDTH + LRU_WIDTH, :], preferred_element_type=F32)
    y = y + jnp.dot(mc_ref[...].astype(BF16), w_ref[A_WIDTH + LRU_WIDTH:, :], preferred_element_type=F32)
    if final_norm:
        y = y * lax.rsqrt(jnp.mean(y * y, axis=-1, keepdims=True) + RMS_EPS) * g_ref[...]
    y_ref[...] = y


def _outproj(x2d, mix_a, mix_b, mix_c, w_out_bf, final_g2, layer, final_norm):
    m = x2d.shape[0]
    tm = min(PROJ_TM, m)
    assert m % tm == 0
    mspec = lambda w: pl.BlockSpec((tm, w), lambda i: (i, 0))
    return pl.pallas_call(
        functools.partial(_outproj_kernel, final_norm=final_norm),
        grid=(m // tm,),
        in_specs=[
            mspec(D_MODEL), mspec(A_WIDTH), mspec(LRU_WIDTH), mspec(C_VWIDTH),
            pl.BlockSpec((None, MIX_WIDTH, D_MODEL), lambda i: (layer, 0, 0)),
            pl.BlockSpec((1, D_MODEL), lambda i: (0, 0)),
        ],
        out_specs=mspec(D_MODEL),
        out_shape=jax.ShapeDtypeStruct((m, D_MODEL), F32),
        compiler_params=_params(("parallel",)),
        name="outproj",
    )(x2d, mix_a, mix_b, mix_c, w_out_bf, final_g2)


def _branch_count(delta):
    delta = np.asarray(delta)
    cnt = np.zeros(delta.shape, np.int32)
    for window, dil in DILATED:
        cnt += ((delta >= 0) & (delta <= window) & (delta % dil == 0)).astype(np.int32)
    return cnt


def _log_count(cnt):
    with np.errstate(divide="ignore"):
        return np.where(cnt > 0, np.log(np.maximum(cnt, 1).astype(np.float64)), -np.inf).astype(np.float32)


def _prompt_bias():
    qi = np.arange(ATT_TQ)[:, None]
    ki = np.arange(ATT_TK)[None, :]
    tabs = []
    for off in range(4):
        b = _log_count(_branch_count(off * ATT_TK + qi - ki))
        tabs.append(np.concatenate([b, b], axis=0))
    return jnp.asarray(np.stack(tabs))


def _attn_prompt_kernel(q_ref, k_ref, v_ref, g_ref, bias_ref, o_ref):
    i = pl.program_id(2)
    lane = lax.broadcasted_iota(jnp.int32, (1, LANES), 1)
    head_a = lane < HEAD_DIM
    q = q_ref[...] * (HEAD_DIM ** -0.5)
    q2 = jnp.concatenate([jnp.where(head_a, q, 0.0), jnp.where(head_a, 0.0, q)], axis=0).astype(BF16)

    def body(j, carry):
        m, l, acc = carry
        k0 = pl.multiple_of(j * ATT_TK, ATT_TK)
        kb = k_ref[pl.ds(k0, ATT_TK), :].astype(BF16)
        vb = v_ref[pl.ds(k0, ATT_TK), :].astype(BF16)
        s = lax.dot_general(q2, kb, (((1,), (1,)), ((), ())), preferred_element_type=F32)
        s = s + bias_ref[jnp.minimum(i - j, 3)]
        m_new = jnp.maximum(m, jnp.max(s, axis=-1, keepdims=True))
        alpha = jnp.exp(m - m_new)
        p = jnp.exp(s - m_new)
        l = alpha * l + jnp.sum(p, axis=-1, keepdims=True)
        acc = alpha * acc + jnp.dot(p.astype(BF16), vb, preferred_element_type=F32)
        return m_new, l, acc

    init = (jnp.full((2 * ATT_TQ, 1), -jnp.inf, F32), jnp.zeros((2 * ATT_TQ, 1), F32),
            jnp.zeros((2 * ATT_TQ, LANES), F32))
    _, l, acc = lax.fori_loop(0, i + 1, body, init)
    o2 = acc / l
    o = jnp.where(head_a, o2[:ATT_TQ], o2[ATT_TQ:])
    o_ref[...] = (o * _silu(g_ref[...])).astype(o_ref.dtype)


def _attn_prompt(z3, bias):
    b, s, _ = z3.shape
    assert ATT_TQ == ATT_TK and s % ATT_TQ == 0 and s <= MAX_WINDOW
    col = lambda base: (lambda bi, p, i: (bi, 0, base // LANES + p))
    return pl.pallas_call(
        _attn_prompt_kernel,
        grid=(b, HEAD_PAIRS, s // ATT_TQ),
        in_specs=[
            pl.BlockSpec((None, ATT_TQ, LANES), lambda bi, p, i: (bi, i, COL_QA // LANES + p)),
            pl.BlockSpec((None, s, LANES), col(COL_KA)),
            pl.BlockSpec((None, s, LANES), col(COL_VA)),
            pl.BlockSpec((None, ATT_TQ, LANES), lambda bi, p, i: (bi, i, COL_GA // LANES + p)),
            pl.BlockSpec((4, 2 * ATT_TQ, ATT_TK), lambda bi, p, i: (0, 0, 0)),
        ],
        out_specs=pl.BlockSpec((None, ATT_TQ, LANES), lambda bi, p, i: (bi, i, p)),
        out_shape=jax.ShapeDtypeStruct((b, s, A_WIDTH), BF16),
        compiler_params=_params(("parallel", "parallel", "arbitrary")),
        name="attn_prompt",
    )(z3, z3, z3, z3, bias)


SA_ROWS = (MAX_WINDOW // 16) * 8
SB_ROWS = (512 // 16) * 8
S_NEW_PAD = LANES


def _sample_bias(t_len):
    rows = np.arange(A_HEADS * t_len) % t_len
    qpos = MAX_WINDOW + rows[:, None]
    ca = np.arange(SA_ROWS)
    pos_a = 16 * (ca // 8) + ca % 8
    cb = np.arange(SB_ROWS)
    pos_b = (MAX_WINDOW - 512) + 16 * (cb // 8) + 8 + cb % 8
    cn = np.arange(S_NEW_PAD)
    pos_n = np.where(cn < t_len, MAX_WINDOW + cn, 10 ** 6)
    pos = np.concatenate([pos_a, pos_b, pos_n])[None, :]
    return jnp.asarray(_log_count(_branch_count(qpos - pos)))


def _attn_sample_kernel(q_ref, k_ref, v_ref, g_ref, ka_ref, kb_ref, va_ref, vb_ref, bias_ref, o_ref, *, t_len):
    lane = lax.broadcasted_iota(jnp.int32, (1, A_WIDTH), 1)
    q = q_ref[...] * (HEAD_DIM ** -0.5)
    heads = [(lane >= h * HEAD_DIM) & (lane < (h + 1) * HEAD_DIM) for h in range(A_HEADS)]
    qbd = jnp.concatenate([jnp.where(hm, q, 0.0) for hm in heads], axis=0).astype(BF16)
    zpad = jnp.zeros((S_NEW_PAD - t_len, A_WIDTH), F32)
    k_new = jnp.concatenate([k_ref[...], zpad], axis=0).astype(BF16)
    v_new = jnp.concatenate([v_ref[...], zpad], axis=0).astype(BF16)
    k_a = ka_ref[...].reshape(SA_ROWS, A_WIDTH).astype(BF16)
    k_b = kb_ref[...].reshape(SB_ROWS, A_WIDTH).astype(BF16)
    nt = (((1,), (1,)), ((), ()))
    s = jnp.concatenate([
        lax.dot_general(qbd, k_a, nt, preferred_element_type=F32),
        lax.dot_general(qbd, k_b, nt, preferred_element_type=F32),
        lax.dot_general(qbd, k_new, nt, preferred_element_type=F32)], axis=1) + bias_ref[...]
    m = jnp.max(s, axis=-1, keepdims=True)
    p = jnp.exp(s - m)
    l = jnp.sum(p, axis=-1, keepdims=True)
    p = p.astype(BF16)
    v_a = va_ref[...].reshape(SA_ROWS, A_WIDTH).astype(BF16)
    v_b = vb_ref[...].reshape(SB_ROWS, A_WIDTH).astype(BF16)
    acc = (jnp.dot(p[:, :SA_ROWS], v_a, preferred_element_type=F32)
           + jnp.dot(p[:, SA_ROWS:SA_ROWS + SB_ROWS], v_b, preferred_element_type=F32)
           + jnp.dot(p[:, SA_ROWS + SB_ROWS:], v_new, preferred_element_type=F32)) / l
    o = jnp.zeros((t_len, A_WIDTH), F32)
    for h in range(A_HEADS):
        o = o + jnp.where(heads[h], acc[h * t_len:(h + 1) * t_len], 0.0)
    o_ref[...] = o * _silu(g_ref[...])


def _attn_sample(zs3, cache_k6, cache_v6, bias, layer):
    b, t_len, _ = zs3.shape
    assert t_len == SUBLANES
    zcol = lambda base: pl.BlockSpec((None, t_len, A_WIDTH), lambda bi: (bi, 0, base // A_WIDTH))
    part_a = pl.BlockSpec((None, None, MAX_WINDOW // 16, 8, A_WIDTH), lambda bi: (layer, bi, 0, 0, 0))
    part_b = pl.BlockSpec((None, None, 512 // 16, 8, A_WIDTH), lambda bi: (layer, bi, (MAX_WINDOW - 512) // 512, 1, 0))
    return pl.pallas_call(
        functools.partial(_attn_sample_kernel, t_len=t_len),
        grid=(b,),
        in_specs=[zcol(COL_QA), zcol(COL_KA), zcol(COL_VA), zcol(COL_GA),
                  part_a, part_b, part_a, part_b,
                  pl.BlockSpec(bias.shape, lambda bi: (0, 0))],
        out_specs=pl.BlockSpec((None, t_len, A_WIDTH), lambda bi: (bi, 0, 0)),
        out_shape=jax.ShapeDtypeStruct((b, t_len, A_WIDTH), F32),
        compiler_params=_params(("parallel",)),
        name="attn_sample",
    )(zs3, zs3, zs3, zs3, cache_k6, cache_k6, cache_v6, cache_v6, bias)


def _neg_expm1(x):
    series = -x * (1.0 + x * (0.5 + x * (1.0 / 6.0 + x * (1.0 / 24.0 + x * (1.0 / 120.0)))))
    return jnp.where(x > -0.1, series, 1.0 - jnp.exp(x))


def _softplus(y):
    return jnp.maximum(y, 0.0) + jnp.log1p(jnp.exp(-jnp.abs(y)))


def _lru_gates(xc, wa_ref, ba_ref, wx_ref, bx_ref, lam_ref):
    xcb = xc.astype(BF16)
    r = jax.nn.sigmoid(jnp.dot(xcb, wa_ref[...], preferred_element_type=F32) + ba_ref[...])
    i = jax.nn.sigmoid(jnp.dot(xcb, wx_ref[...], preferred_element_type=F32) + bx_ref[...])
    log_a = -LRU_C * r * _softplus(-lam_ref[...])
    a = jnp.exp(log_a)
    u = jnp.sqrt(_neg_expm1(2.0 * log_a)) * (i * xc)
    return a, u


def _scan8(a, u, axis, row):
    for sh in (1, 2, 4):
        keep = row >= sh
        a_prev = jnp.where(keep, pltpu.roll(a, sh, axis), 1.0)
        u_prev = jnp.where(keep, pltpu.roll(u, sh, axis), 0.0)
        u = a * u_prev + u
        a = a * a_prev
    return a, u


def _lru_prompt_kernel(xb_ref, gb_ref, cw_ref, cb_ref, wa_ref, ba_ref, wx_ref, bx_ref, lam_ref,
                       o_ref, hlast_ref, xp_scr, a_scr, u_scr, hc_scr, *, tile):
    t = pl.program_id(1)

    @pl.when(t == 0)
    def _():
        xp_scr[0:SUBLANES, :] = jnp.zeros((SUBLANES, LRU_WIDTH), F32)
        hc_scr[...] = jnp.zeros_like(hc_scr)

    xp_scr[SUBLANES:SUBLANES + tile, :] = xb_ref[...]
    cw = cw_ref[...]
    xc = cb_ref[...]
    for j in range(CONV_W):
        off = SUBLANES - (CONV_W - 1) + j
        xc = xc + xp_scr[off:off + tile, :] * cw[j:j + 1, :]
    xp_scr[0:SUBLANES, :] = xp_scr[tile:tile + SUBLANES, :]

    a, u = _lru_gates(xc, wa_ref, ba_ref, wx_ref, bx_ref, lam_ref)
    row = lax.broadcasted_iota(jnp.int32, (tile, 1), 0) % SUBLANES
    a, u = _scan8(a, u, 0, row)
    a_scr[...] = a
    u_scr[...] = u

    def body(gi, hc):
        r0 = pl.multiple_of(gi * SUBLANES, SUBLANES)
        h = a_scr[pl.ds(r0, SUBLANES), :] * hc + u_scr[pl.ds(r0, SUBLANES), :]
        u_scr[pl.ds(r0, SUBLANES), :] = h
        return jnp.broadcast_to(h[SUBLANES - 1:SUBLANES, :], (SUBLANES, LRU_WIDTH))

    hc = lax.fori_loop(0, tile // SUBLANES, body, hc_scr[...])
    hc_scr[...] = hc
    o_ref[...] = (u_scr[...] * _silu(gb_ref[...])).astype(o_ref.dtype)

    @pl.when(t == pl.num_programs(1) - 1)
    def _():
        hlast_ref[...] = hc[0:1, :]


def _lru_weight_specs(layer, nargs):
    def spec(shape):
        idx = (layer,) + (0,) * len(shape)
        if nargs == 1:
            return pl.BlockSpec((None,) + shape, lambda a: idx)
        return pl.BlockSpec((None,) + shape, lambda a, b: idx)
    return [spec((CONV_W, LRU_WIDTH)), spec((1, LRU_WIDTH)),
            spec((LRU_WIDTH, LRU_WIDTH)), spec((1, LRU_WIDTH)),
            spec((LRU_WIDTH, LRU_WIDTH)), spec((1, LRU_WIDTH)), spec((1, LRU_WIDTH))]


def _lru_prompt(z3, lru_w, layer):
    b, s, _ = z3.shape
    tile = min(LRU_TILE, s)
    assert s % tile == 0
    return pl.pallas_call(
        functools.partial(_lru_prompt_kernel, tile=tile),
        grid=(b, s // tile),
        in_specs=[pl.BlockSpec((None, tile, LRU_WIDTH), lambda bi, t: (bi, t, COL_XB // LRU_WIDTH)),
                  pl.BlockSpec((None, tile, LRU_WIDTH), lambda bi, t: (bi, t, COL_GB // LRU_WIDTH))]
        + _lru_weight_specs(layer, 2),
        out_specs=[pl.BlockSpec((None, tile, LRU_WIDTH), lambda bi, t: (bi, t, 0)),
                   pl.BlockSpec((None, 1, LRU_WIDTH), lambda bi, t: (bi, 0, 0))],
        out_shape=[jax.ShapeDtypeStruct((b, s, LRU_WIDTH), BF16),
                   jax.ShapeDtypeStruct((b, 1, LRU_WIDTH), F32)],
        scratch_shapes=[pltpu.VMEM((tile + 2 * SUBLANES, LRU_WIDTH), F32),
                        pltpu.VMEM((tile, LRU_WIDTH), F32),
                        pltpu.VMEM((tile, LRU_WIDTH), F32),
                        pltpu.VMEM((SUBLANES, LRU_WIDTH), F32)],
        compiler_params=_params(("parallel", "arbitrary")),
        name="lru_prompt",
    )(z3, z3, *lru_w)


def _lru_sample_kernel(xb_ref, gb_ref, cs_ref, h0_ref, cw_ref, cb_ref, wa_ref, ba_ref, wx_ref, bx_ref, lam_ref,
                       o_ref, hlast_ref):
    x = xb_ref[...]
    nb, t_len, _ = x.shape
    cs = cs_ref[...]
    row = lax.broadcasted_iota(jnp.int32, (1, t_len, 1), 1)
    cw = cw_ref[...]
    xc = cb_ref[...] + x * cw[CONV_W - 1:CONV_W, :]
    for back in range(1, CONV_W):
        xs = pltpu.roll(x, back, 1)
        for r in range(back):
            src = CONV_W - 1 - back + r
            xs = jnp.where(row == r, cs[:, src:src + 1, :], xs)
        xc = xc + xs * cw[CONV_W - 1 - back:CONV_W - back, :]
    a, u = _lru_gates(xc.reshape(nb * t_len, LRU_WIDTH), wa_ref, ba_ref, wx_ref, bx_ref, lam_ref)
    a = a.reshape(nb, t_len, LRU_WIDTH)
    u = u.reshape(nb, t_len, LRU_WIDTH)
    a, u = _scan8(a, u, 1, row)
    h = a * h0_ref[...] + u
    o_ref[...] = h * _silu(gb_ref[...])
    hlast_ref[...] = h[:, t_len - 1:t_len, :]


def _lru_sample(zs3, state_conv, state_lru4, lru_w, layer):
    b, t_len, _ = zs3.shape
    assert t_len == SUBLANES
    return pl.pallas_call(
        _lru_sample_kernel,
        grid=(1,),
        in_specs=[pl.BlockSpec((b, t_len, LRU_WIDTH), lambda i: (0, 0, COL_XB // LRU_WIDTH)),
                  pl.BlockSpec((b, t_len, LRU_WIDTH), lambda i: (0, 0, COL_GB // LRU_WIDTH)),
                  pl.BlockSpec((None, b, CONV_W - 1, LRU_WIDTH), lambda i: (layer, 0, 0, 0)),
                  pl.BlockSpec((None, b, 1, LRU_WIDTH), lambda i: (layer, 0, 0, 0))]
        + _lru_weight_specs(layer, 1),
        out_specs=[pl.BlockSpec((b, t_len, LRU_WIDTH), lambda i: (0, 0, 0)),
                   pl.BlockSpec((b, 1, LRU_WIDTH), lambda i: (0, 0, 0))],
        out_shape=[jax.ShapeDtypeStruct((b, t_len, LRU_WIDTH), F32),
                   jax.ShapeDtypeStruct((b, 1, LRU_WIDTH), F32)],
        compiler_params=_params(("arbitrary",)),
        name="lru_sample",
    )(zs3, zs3, state_conv, state_lru4, *lru_w)


def _split3(x):
    hi = x.astype(BF16)
    r = x - hi.astype(F32)
    mid = r.astype(BF16)
    lo = (r - mid.astype(F32)).astype(BF16)
    return hi, mid, lo


def _cumsum_rows(x):
    n = x.shape[0]
    row = lax.broadcasted_iota(jnp.int32, (n, 1), 0)
    if n == SUBLANES:
        for sh in (1, 2, 4):
            x = x + jnp.where(row >= sh, pltpu.roll(x, sh, 0), 0.0)
        return x
    col = lax.broadcasted_iota(jnp.int32, (1, n), 1)
    tri = (row >= col).astype(BF16)
    out = None
    for part in _split3(x):
        term = jnp.dot(tri, part, preferred_element_type=F32)
        out = term if out is None else out + term
    return out


def _pad_rows(x, n):
    if x.shape[0] == n:
        return x
    return jnp.concatenate([x, jnp.zeros((n - x.shape[0], x.shape[1]), x.dtype)], axis=0)


def _lower_bound(logits, layer):
    e = jnp.exp(logits - jnp.max(logits, axis=0, keepdims=True))
    den = jnp.sum(e, axis=0, keepdims=True)
    num = jnp.zeros_like(den)
    for r in range(1, layer + 1):
        num = num + e[r:r + 1, :]
    return num / den


def _hgrn_chunk(qc2, fc2, v, gc, st, lb2, cg):
    n = v.shape[0]
    lane = lax.broadcasted_iota(jnp.int32, (1, LANES), 1)
    head_a = lane < C_VDIM
    row = lax.broadcasted_iota(jnp.int32, (n, 1), 0)
    sub = row % SUBLANES
    col = lax.broadcasted_iota(jnp.int32, (1, n), 1)
    nt = (((1,), (1,)), ((), ()))
    st_b = st.astype(BF16)
    v_b = v.astype(BF16)
    outs, decays, k_ends = [], [], []
    for a in range(2):
        qc = qc2[:, a * C_KDIM:(a + 1) * C_KDIM]
        fc = fc2[:, a * C_KDIM:(a + 1) * C_KDIM]
        lb = lb2[:, a * C_KDIM:(a + 1) * C_KDIM]
        q = _silu(qc) * (C_KDIM ** -0.5)
        g = lb + (1.0 - lb) * jax.nn.sigmoid(fc)
        k = 1.0 - g
        b = _cumsum_rows(jnp.log(g))
        o = lax.dot_general((q * jnp.exp(b)).astype(BF16), st_b, nt, preferred_element_type=F32)
        o = o + jnp.sum(q * k, axis=-1, keepdims=True) * v
        for d in range(1, SUBLANES):
            w = q * pltpu.roll(k, d, 0) * jnp.exp(jnp.minimum(b - pltpu.roll(b, d, 0), 0.0))
            a_d = jnp.where(sub >= d, jnp.sum(w, axis=-1, keepdims=True), 0.0)
            o = o + a_d * pltpu.roll(v, d, 0)
        if n > SUBLANES:
            blocks = [jnp.zeros((SUBLANES, n), F32)]
            for i in range(1, n // SUBLANES):
                lo = i * SUBLANES
                ref = b[lo - 1:lo, :]
                kt = (k * jnp.exp(jnp.minimum(ref - b, 0.0))).astype(BF16)
                qt = (q[lo:lo + SUBLANES] * jnp.exp(b[lo:lo + SUBLANES] - ref)).astype(BF16)
                blk = lax.dot_general(qt, kt, nt, preferred_element_type=F32)
                blocks.append(jnp.where(col < lo, blk, 0.0))
            attn = jnp.concatenate(blocks, axis=0).astype(BF16)
            o = o + jnp.dot(attn, v_b, preferred_element_type=F32)
        outs.append(o)
        last = b[n - 1:n, :]
        decays.append(jnp.exp(last))
        k_ends.append(k * jnp.exp(last - b))
    o = jnp.where(head_a, outs[0], outs[1])
    v_t = _pad_rows(v, LANES).T.astype(BF16)
    top = lax.broadcasted_iota(jnp.int32, (LANES, 1), 0) < C_VDIM
    new_a = st * decays[0] + jnp.dot(v_t, _pad_rows(k_ends[0], LANES).astype(BF16), preferred_element_type=F32)
    new_b = st * decays[1] + jnp.dot(v_t, _pad_rows(k_ends[1], LANES).astype(BF16), preferred_element_type=F32)
    st_new = jnp.where(top, new_a, new_b)
    sq = o * o
    ms_a = jnp.sum(jnp.where(head_a, sq, 0.0), axis=-1, keepdims=True) * (1.0 / C_VDIM)
    ms_b = jnp.sum(jnp.where(head_a, 0.0, sq), axis=-1, keepdims=True) * (1.0 / C_VDIM)
    inv = jnp.where(head_a, lax.rsqrt(ms_a + RMS_EPS), lax.rsqrt(ms_b + RMS_EPS))
    return o * inv * cg * _silu(gc), st_new


def _hgrn_prompt_kernel(qc_ref, fc_ref, ic_ref, gc_ref, lbl_ref, cg_ref, o_ref, st_ref, st_scr, *, layer, tile):
    t = pl.program_id(2)

    @pl.when(t == 0)
    def _():
        st_scr[...] = jnp.zeros_like(st_scr)

    lb2 = _lower_bound(lbl_ref[...], layer)
    cg = cg_ref[...]

    def body(c, carry):
        r0 = pl.multiple_of(c * HGRN_CHUNK, HGRN_CHUNK)
        rows = pl.ds(r0, HGRN_CHUNK)
        out, st_new = _hgrn_chunk(qc_ref[rows, :], fc_ref[rows, :], ic_ref[rows, :], gc_ref[rows, :],
                                  st_scr[...], lb2, cg)
        o_ref[rows, :] = out.astype(o_ref.dtype)
        st_scr[...] = st_new
        return carry

    lax.fori_loop(0, tile // HGRN_CHUNK, body, 0)

    @pl.when(t == pl.num_programs(2) - 1)
    def _():
        st_ref[...] = st_scr[...]


def _hgrn_prompt(z3, lb_logits, cnorm3, layer):
    b, s, _ = z3.shape
    tile = min(HGRN_TILE, s)
    assert s % tile == 0 and tile % HGRN_CHUNK == 0
    kw, vw = 2 * C_KDIM, 2 * C_VDIM
    return pl.pallas_call(
        functools.partial(_hgrn_prompt_kernel, layer=layer, tile=tile),
        grid=(b, HEAD_PAIRS, s // tile),
        in_specs=[pl.BlockSpec((None, tile, kw), lambda bi, p, t: (bi, t, COL_QC // kw + p)),
                  pl.BlockSpec((None, tile, kw), lambda bi, p, t: (bi, t, COL_FC // kw + p)),
                  pl.BlockSpec((None, tile, vw), lambda bi, p, t: (bi, t, COL_IC // vw + p)),
                  pl.BlockSpec((None, tile, vw), lambda bi, p, t: (bi, t, COL_GC // vw + p)),
                  pl.BlockSpec((DEPTH, kw), lambda bi, p, t: (0, p)),
                  pl.BlockSpec((None, 1, vw), lambda bi, p, t: (layer, 0, p))],
        out_specs=[pl.BlockSpec((None, tile, vw), lambda bi, p, t: (bi, t, p)),
                   pl.BlockSpec((None, None, LANES, LANES), lambda bi, p, t: (bi, p, 0, 0))],
        out_shape=[jax.ShapeDtypeStruct((b, s, C_VWIDTH), BF16),
                   jax.ShapeDtypeStruct((b, HEAD_PAIRS, LANES, LANES), F32)],
        scratch_shapes=[pltpu.VMEM((LANES, LANES), F32)],
        compiler_params=_params(("parallel", "parallel", "arbitrary")),
        name="hgrn_prompt",
    )(z3, z3, z3, z3, lb_logits, cnorm3)


def _hgrn_sample_kernel(qc_ref, fc_ref, ic_ref, gc_ref, st_in_ref, lbl_ref, cg_ref, o_ref, st_ref, *, layer):
    lb2 = _lower_bound(lbl_ref[...], layer)
    out, st_new = _hgrn_chunk(qc_ref[...], fc_ref[...], ic_ref[...], gc_ref[...], st_in_ref[...], lb2, cg_ref[...])
    o_ref[...] = out
    st_ref[...] = st_new


def _hgrn_sample(zs3, state_t, lb_logits, cnorm3, layer):
    b, t_len, _ = zs3.shape
    assert t_len == SUBLANES
    kw, vw = 2 * C_KDIM, 2 * C_VDIM
    return pl.pallas_call(
        functools.partial(_hgrn_sample_kernel, layer=layer),
        grid=(b, HEAD_PAIRS),
        in_specs=[pl.BlockSpec((None, t_len, kw), lambda bi, p: (bi, 0, COL_QC // kw + p)),
                  pl.BlockSpec((None, t_len, kw), lambda bi, p: (bi, 0, COL_FC // kw + p)),
                  pl.BlockSpec((None, t_len, vw), lambda bi, p: (bi, 0, COL_IC // vw + p)),
                  pl.BlockSpec((None, t_len, vw), lambda bi, p: (bi, 0, COL_GC // vw + p)),
                  pl.BlockSpec((None, None, None, LANES, LANES), lambda bi, p: (layer, bi, p, 0, 0)),
                  pl.BlockSpec((DEPTH, kw), lambda bi, p: (0, p)),
                  pl.BlockSpec((None, 1, vw), lambda bi, p: (layer, 0, p))],
        out_specs=[pl.BlockSpec((None, t_len, vw), lambda bi, p: (bi, 0, p)),
                   pl.BlockSpec((None, None, LANES, LANES), lambda bi, p: (bi, p, 0, 0))],
        out_shape=[jax.ShapeDtypeStruct((b, t_len, C_VWIDTH), F32),
                   jax.ShapeDtypeStruct((b, HEAD_PAIRS, LANES, LANES), F32)],
        compiler_params=_params(("parallel", "parallel")),
        name="hgrn_sample",
    )(zs3, zs3, zs3, zs3, state_t, lb_logits, cnorm3)


def _block_diag(w):
    depth, nb, d, _ = w.shape
    eye = jnp.eye(nb, dtype=w.dtype)
    full = jnp.einsum("lhij,hg->lhigj", w, eye)
    return full.reshape(depth, nb * d, nb * d).astype(BF16)


def _state_to_pairs(st):
    lead = st.shape[:-3]
    t = jnp.swapaxes(st, -1, -2)
    return t.reshape(lead + (HEAD_PAIRS, 2 * C_VDIM, C_KDIM))


def _pairs_to_state(sp):
    lead = sp.shape[:-3]
    t = sp.reshape(lead + (C_HEADS, C_VDIM, C_KDIM))
    return jnp.swapaxes(t, -1, -2)


def kernel(x_prompt, x_sample, cache_k_win, cache_v_win, state_conv, state_lru, state_hgrn, norm_g, w_in, conv_w,
           conv_b, lru_w_a, lru_b_a, lru_w_x, lru_b_x, lru_lambda, hgrn_lb_logits, hgrn_norm_g, w_out, final_norm_g):
    bp, sp, _ = x_prompt.shape
    bs, ts, _ = x_sample.shape
    w_buf = cache_k_win.shape[2]
    assert w_buf == MAX_WINDOW and sp <= MAX_WINDOW

    w_in_bf = w_in.astype(BF16)
    w_out_bf = w_out.astype(BF16)
    norm_g3 = norm_g.reshape(DEPTH, 1, D_MODEL)
    final_g2 = final_norm_g.reshape(1, D_MODEL)
    row3 = lambda p: p.reshape(DEPTH, 1, -1)
    lru_w = (conv_w, row3(conv_b), _block_diag(lru_w_a), row3(lru_b_a), _block_diag(lru_w_x), row3(lru_b_x),
             row3(lru_lambda))
    cnorm3 = row3(hgrn_norm_g)
    cache_k6 = cache_k_win.reshape(DEPTH, bs, w_buf // 16, 16, A_WIDTH)
    cache_v6 = cache_v_win.reshape(DEPTH, bs, w_buf // 16, 16, A_WIDTH)
    state_lru4 = state_lru.reshape(DEPTH, bs, 1, LRU_WIDTH)
    state_t = _state_to_pairs(state_hgrn)
    bias_p = _prompt_bias()
    bias_s = _sample_bias(ts)

    xp = x_prompt.reshape(bp * sp, D_MODEL)
    xs = x_sample.reshape(bs * ts, D_MODEL)
    outs = {k: [] for k in ("kp", "vp", "ks", "vs", "cp", "cs", "lp", "ls", "hp", "hs")}
    for layer in range(DEPTH):
        last = layer == DEPTH - 1
        z = _inproj(xp, norm_g3, w_in_bf, layer)
        z3 = z.reshape(bp, sp, IN_WIDTH)
        mix_a = _attn_prompt(z3, bias_p)
        mix_b, h_last = _lru_prompt(z3, lru_w, layer)
        mix_c, st_p = _hgrn_prompt(z3, hgrn_lb_logits, cnorm3, layer)
        xp = _outproj(xp, mix_a.reshape(bp * sp, -1), mix_b.reshape(bp * sp, -1), mix_c.reshape(bp * sp, -1),
                      w_out_bf, final_g2, layer, last)
        outs["kp"].append(z3[:, :, COL_KA:COL_KA + A_WIDTH].reshape(bp, sp, A_HEADS, HEAD_DIM))
        outs["vp"].append(z3[:, :, COL_VA:COL_VA + A_WIDTH].reshape(bp, sp, A_HEADS, HEAD_DIM))
        outs["cp"].append(z3[:, sp - (CONV_W - 1):, COL_XB:COL_XB + LRU_WIDTH])
        outs["lp"].append(h_last.reshape(bp, LRU_WIDTH))
        outs["hp"].append(_pairs_to_state(st_p))
        zs = _inproj(xs, norm_g3, w_in_bf, layer)
        zs3 = zs.reshape(bs, ts, IN_WIDTH)
        smix_a = _attn_sample(zs3, cache_k6, cache_v6, bias_s, layer)
        smix_b, sh_last = _lru_sample(zs3, state_conv, state_lru4, lru_w, layer)
        smix_c, st_s = _hgrn_sample(zs3, state_t, hgrn_lb_logits, cnorm3, layer)
        xs = _outproj(xs, smix_a.reshape(bs * ts, -1), smix_b.reshape(bs * ts, -1), smix_c.reshape(bs * ts, -1),
                      w_out_bf, final_g2, layer, last)
        outs["ks"].append(zs3[:, :, COL_KA:COL_KA + A_WIDTH].reshape(bs, ts, A_HEADS, HEAD_DIM))
        outs["vs"].append(zs3[:, :, COL_VA:COL_VA + A_WIDTH].reshape(bs, ts, A_HEADS, HEAD_DIM))
        outs["cs"].append(zs3[:, ts - (CONV_W - 1):, COL_XB:COL_XB + LRU_WIDTH])
        outs["ls"].append(sh_last.reshape(bs, LRU_WIDTH))
        outs["hs"].append(_pairs_to_state(st_s))
    st = lambda k: jnp.stack(outs[k])
    return (xp.reshape(bp, sp, D_MODEL), xs.reshape(bs, ts, D_MODEL), st("kp"), st("vp"), st("ks"), st("vs"),
            st("cp"), st("cs"), st("lp"), st("ls"), st("hp"), st("hs"))
```

```python
import functools

import jax
import jax.numpy as jnp
import numpy as np
from jax import lax
from jax.experimental import pallas as pl
from jax.experimental.pallas import tpu as pltpu

F32 = jnp.float32
BF16 = jnp.bfloat16

D_MODEL = 1024
DEPTH = 4
HEAD_DIM = 64
A_HEADS = 6
A_WIDTH = A_HEADS * HEAD_DIM
DILATED = ((128, 1), (512, 4), (2048, 16))
MAX_WINDOW = 2048
LRU_BLOCKS = 6
LRU_BLOCK_DIM = 64
LRU_WIDTH = LRU_BLOCKS * LRU_BLOCK_DIM
CONV_W = 4
LRU_C = 8.0
C_HEADS = 6
C_KDIM = 128
C_VDIM = 64
C_KWIDTH = C_HEADS * C_KDIM
C_VWIDTH = C_HEADS * C_VDIM
MIX_WIDTH = A_WIDTH + LRU_WIDTH + C_VWIDTH
IN_WIDTH = 4 * A_WIDTH + 2 * LRU_WIDTH + 2 * C_KWIDTH + 2 * C_VWIDTH
RMS_EPS = 1e-6
LOG2_E = 1.4426950408889634

LANES = 128
SUBLANES = 8
HEAD_PAIRS = A_HEADS // 2
VMEM_LIMIT = 56 * 1024 * 1024

COL_QA, COL_KA, COL_VA, COL_GA = 0, A_WIDTH, 2 * A_WIDTH, 3 * A_WIDTH
COL_XB = 4 * A_WIDTH
COL_GB = COL_XB + LRU_WIDTH
COL_QC = COL_GB + LRU_WIDTH
COL_FC = COL_QC + C_KWIDTH
COL_IC = COL_FC + C_KWIDTH
COL_GC = COL_IC + C_VWIDTH

ATT_TQ = 256
ATT_TK = 512
HGRN_CHUNK = 64
HGRN_TILE = 512
LRU_TILE = 512
PROJ_TM = 256


def _silu(x):
    return x * jax.nn.sigmoid(x)


def _params(sem, **kw):
    return pltpu.CompilerParams(dimension_semantics=sem, vmem_limit_bytes=VMEM_LIMIT, **kw)


def _inproj_kernel(x_ref, g_ref, w_ref, z_ref, *, n_chunk):
    x = x_ref[...]
    h = (x * lax.rsqrt(jnp.mean(x * x, axis=-1, keepdims=True) + RMS_EPS) * g_ref[...]).astype(BF16)
    for n in range(IN_WIDTH // n_chunk):
        sl = slice(n * n_chunk, (n + 1) * n_chunk)
        z_ref[:, sl] = jnp.dot(h, w_ref[:, sl], preferred_element_type=F32)


def _inproj(x2d, norm_g3, w_in_bf, layer):
    m = x2d.shape[0]
    tm = min(PROJ_TM, m)
    assert m % tm == 0
    return pl.pallas_call(
        functools.partial(_inproj_kernel, n_chunk=512),
        grid=(m // tm,),
        in_specs=[
            pl.BlockSpec((tm, D_MODEL), lambda i: (i, 0)),
            pl.BlockSpec((None, 1, D_MODEL), lambda i: (layer, 0, 0)),
            pl.BlockSpec((None, D_MODEL, IN_WIDTH), lambda i: (layer, 0, 0)),
        ],
        out_specs=pl.BlockSpec((tm, IN_WIDTH), lambda i: (i, 0)),
        out_shape=jax.ShapeDtypeStruct((m, IN_WIDTH), F32),
        compiler_params=_params(("parallel",)),
        name="inproj",
    )(x2d, norm_g3, w_in_bf)


def _outproj_kernel(x_ref, ma_ref, mb_ref, mc_ref, w_ref, g_ref, y_ref, *, final_norm):
    y = x_ref[...]
    y = y + jnp.dot(ma_ref[...].astype(BF16), w_ref[0:A_WIDTH, :], preferred_element_type=F32)
    y = y + jnp.dot(mb_ref[...].astype(BF16), w_ref[A_WIDTH:A_WIDTH + LRU_WIDTH, :], preferred_element_type=F32)
    y = y + jnp.dot(mc_ref[...].astype(BF16), w_ref[A_WIDTH + LRU_WIDTH:, :], preferred_element_type=F32)
    if final_norm:
        y = y * lax.rsqrt(jnp.mean(y * y, axis=-1, keepdims=True) + RMS_EPS) * g_ref[...]
    y_ref[...] = y


def _outproj(x2d, mix_a, mix_b, mix_c, w_out_bf, final_g2, layer, final_norm):
    m = x2d.shape[0]
    tm = min(PROJ_TM, m)
    assert m % tm == 0
    mspec = lambda w: pl.BlockSpec((tm, w), lambda i: (i, 0))
    return pl.pallas_call(
        functools.partial(_outproj_kernel, final_norm=final_norm),
        grid=(m // tm,),
        in_specs=[
            mspec(D_MODEL), mspec(A_WIDTH), mspec(LRU_WIDTH), mspec(C_VWIDTH),
            pl.BlockSpec((None, MIX_WIDTH, D_MODEL), lambda i: (layer, 0, 0)),
            pl.BlockSpec((1, D_MODEL), lambda i: (0, 0)),
        ],
        out_specs=mspec(D_MODEL),
        out_shape=jax.ShapeDtypeStruct((m, D_MODEL), F32),
        compiler_params=_params(("parallel",)),
        name="outproj",
    )(x2d, mix_a, mix_b, mix_c, w_out_bf, final_g2)


def _branch_count(delta):
    delta = np.asarray(delta)
    cnt = np.zeros(delta.shape, np.int32)
    for window, dil in DILATED:
        cnt += ((delta >= 0) & (delta <= window) & (delta % dil == 0)).astype(np.int32)
    return cnt


def _log_count(cnt):
    with np.errstate(divide="ignore"):
        return np.where(cnt > 0, np.log(np.maximum(cnt, 1).astype(np.float64)), -np.inf).astype(np.float32)


ATT_FAR_OFF = (DILATED[1][0] + ATT_TK - 1) // ATT_TQ + 1


def _prompt_bias():
    qi = np.arange(ATT_TQ)[None, :]
    ki = np.arange(ATT_TK)[:, None]
    tabs = []
    for off in range(ATT_FAR_OFF + 1):
        b = _log_count(_branch_count(off * ATT_TQ + qi - ki)) * np.float32(LOG2_E)
        tabs.append(np.concatenate([b, b], axis=1))
    return jnp.asarray(np.stack(tabs))


def _attn_prompt_kernel(q_ref, k_ref, v_ref, g_ref, bias_ref, o_ref):
    i = pl.program_id(2)
    lane = lax.broadcasted_iota(jnp.int32, (1, LANES), 1)
    head_a = lane < HEAD_DIM
    q = q_ref[...] * (HEAD_DIM ** -0.5 * LOG2_E)
    q2 = jnp.concatenate([jnp.where(head_a, q, 0.0), jnp.where(head_a, 0.0, q)], axis=0).astype(BF16)

    def body(j, carry):
        m, l, acc = carry
        k0 = pl.multiple_of(j * ATT_TK, ATT_TK)
        kb = k_ref[pl.ds(k0, ATT_TK), :].astype(BF16)
        vb = v_ref[pl.ds(k0, ATT_TK), :].astype(BF16)
        s = lax.dot_general(kb, q2, (((1,), (1,)), ((), ())), preferred_element_type=F32)
        s = s + bias_ref[jnp.minimum(i - j * (ATT_TK // ATT_TQ), ATT_FAR_OFF)]
        m_new = jnp.maximum(m, jnp.max(s, axis=0, keepdims=True))
        alpha = jnp.exp2(m - m_new)
        p = jnp.exp2(s - m_new)
        l = alpha * l + jnp.sum(p, axis=0, keepdims=True)
        pv = lax.dot_general(vb, p.astype(BF16), (((0,), (0,)), ((), ())), preferred_element_type=F32)
        return m_new, l, alpha * acc + pv

    init = (jnp.full((1, 2 * ATT_TQ), -jnp.inf, F32), jnp.zeros((1, 2 * ATT_TQ), F32),
            jnp.zeros((LANES, 2 * ATT_TQ), F32))
    _, l, acc = lax.fori_loop(0, (i * ATT_TQ) // ATT_TK + 1, body, init)
    o2 = (acc / l).T
    o = jnp.where(head_a, o2[:ATT_TQ], o2[ATT_TQ:])
    o_ref[...] = (o * _silu(g_ref[...])).astype(o_ref.dtype)


def _attn_prompt(z3, bias):
    b, s, _ = z3.shape
    assert ATT_TK % ATT_TQ == 0 and s % ATT_TK == 0 and s <= MAX_WINDOW and ATT_TQ % DILATED[2][1] == 0
    col = lambda base: (lambda bi, p, i: (bi, 0, base // LANES + p))
    return pl.pallas_call(
        _attn_prompt_kernel,
        grid=(b, HEAD_PAIRS, s // ATT_TQ),
        in_specs=[
            pl.BlockSpec((None, ATT_TQ, LANES), lambda bi, p, i: (bi, i, COL_QA // LANES + p)),
            pl.BlockSpec((None, s, LANES), col(COL_KA)),
            pl.BlockSpec((None, s, LANES), col(COL_VA)),
            pl.BlockSpec((None, ATT_TQ, LANES), lambda bi, p, i: (bi, i, COL_GA // LANES + p)),
            pl.BlockSpec((ATT_FAR_OFF + 1, ATT_TK, 2 * ATT_TQ), lambda bi, p, i: (0, 0, 0)),
        ],
        out_specs=pl.BlockSpec((None, ATT_TQ, LANES), lambda bi, p, i: (bi, i, p)),
        out_shape=jax.ShapeDtypeStruct((b, s, A_WIDTH), BF16),
        compiler_params=_params(("parallel", "parallel", "arbitrary")),
        name="attn_prompt",
    )(z3, z3, z3, z3, bias)


def _sample_bias(t_len):
    qpos = MAX_WINDOW + np.arange(t_len)[:, None]
    bias_c = _log_count(_branch_count(qpos - np.arange(MAX_WINDOW)[None, :]))
    bias_n = _log_count(_branch_count(qpos - (MAX_WINDOW + np.arange(t_len)[None, :])))
    return jnp.asarray(bias_c), jnp.asarray(bias_n)


def _split_heads(x):
    return jnp.stack([x[:, h * HEAD_DIM:(h + 1) * HEAD_DIM] for h in range(A_HEADS)], axis=0)


def _attn_sample_kernel(q_ref, k_ref, v_ref, g_ref, kt_ref, vt_ref, bias_c_ref, bias_n_ref, o_ref):
    q3 = _split_heads(q_ref[...] * (HEAD_DIM ** -0.5))
    kn3 = _split_heads(k_ref[...])
    vn3 = _split_heads(v_ref[...])
    bdot = lambda a, b, ca, cb: lax.dot_general(a, b, (((ca,), (cb,)), ((0,), (0,))), preferred_element_type=F32)
    s_c = bdot(q3, kt_ref[...], 2, 1) + bias_c_ref[...][None]
    s_n = bdot(q3, kn3, 2, 2) + bias_n_ref[...][None]
    m = jnp.maximum(jnp.max(s_c, axis=-1, keepdims=True), jnp.max(s_n, axis=-1, keepdims=True))
    p_c = jnp.exp(s_c - m)
    p_n = jnp.exp(s_n - m)
    l = jnp.sum(p_c, axis=-1, keepdims=True) + jnp.sum(p_n, axis=-1, keepdims=True)
    acc = bdot(p_c, vt_ref[...], 2, 2) + bdot(p_n, vn3, 2, 1)
    o3 = acc / l
    o = jnp.concatenate([o3[h] for h in range(A_HEADS)], axis=-1)
    o_ref[...] = o * _silu(g_ref[...])


def _attn_sample(zs3, cache_kt, cache_vt, bias, layer):
    b, t_len, _ = zs3.shape
    w_buf = cache_kt.shape[-1]
    zcol = lambda base: pl.BlockSpec((None, t_len, A_WIDTH), lambda bi: (bi, 0, base // A_WIDTH))
    cache = pl.BlockSpec((None, None, A_HEADS, HEAD_DIM, w_buf), lambda bi: (layer, bi, 0, 0, 0))
    bias_c, bias_n = bias
    return pl.pallas_call(
        _attn_sample_kernel,
        grid=(b,),
        in_specs=[zcol(COL_QA), zcol(COL_KA), zcol(COL_VA), zcol(COL_GA), cache, cache,
                  pl.BlockSpec(bias_c.shape, lambda bi: (0, 0)), pl.BlockSpec(bias_n.shape, lambda bi: (0, 0))],
        out_specs=pl.BlockSpec((None, t_len, A_WIDTH), lambda bi: (bi, 0, 0)),
        out_shape=jax.ShapeDtypeStruct((b, t_len, A_WIDTH), F32),
        compiler_params=_params(("parallel",)),
        name="attn_sample",
    )(zs3, zs3, zs3, zs3, cache_kt, cache_vt, bias_c, bias_n)


def _neg_expm1(x):
    series = -x * (1.0 + x * (0.5 + x * (1.0 / 6.0 + x * (1.0 / 24.0 + x * (1.0 / 120.0)))))
    return jnp.where(x > -0.1, series, 1.0 - jnp.exp(x))


def _softplus(y):
    return jnp.maximum(y, 0.0) + jnp.log1p(jnp.exp(-jnp.abs(y)))


def _lru_gates(xc, wa_ref, ba_ref, wx_ref, bx_ref, lam_ref):
    xcb = xc.astype(BF16)
    r = jax.nn.sigmoid(jnp.dot(xcb, wa_ref[...], preferred_element_type=F32) + ba_ref[...])
    i = jax.nn.sigmoid(jnp.dot(xcb, wx_ref[...], preferred_element_type=F32) + bx_ref[...])
    log_a = -LRU_C * r * _softplus(-lam_ref[...])
    a = jnp.exp(log_a)
    u = jnp.sqrt(_neg_expm1(2.0 * log_a)) * (i * xc)
    return a, u


def _scan8(a, u, axis, row):
    for sh in (1, 2, 4):
        keep = row >= sh
        a_prev = jnp.where(keep, pltpu.roll(a, sh, axis), 1.0)
        u_prev = jnp.where(keep, pltpu.roll(u, sh, axis), 0.0)
        u = a * u_prev + u
        a = a * a_prev
    return a, u


def _lru_prompt_kernel(xb_ref, gb_ref, cw_ref, cb_ref, wa_ref, ba_ref, wx_ref, bx_ref, lam_ref,
                       o_ref, hlast_ref, xp_scr, a_scr, u_scr, hc_scr, *, tile):
    t = pl.program_id(1)

    @pl.when(t == 0)
    def _():
        xp_scr[0:SUBLANES, :] = jnp.zeros((SUBLANES, LRU_WIDTH), F32)
        hc_scr[...] = jnp.zeros_like(hc_scr)

    xp_scr[SUBLANES:SUBLANES + tile, :] = xb_ref[...]
    cw = cw_ref[...]
    xc = cb_ref[...]
    for j in range(CONV_W):
        off = SUBLANES - (CONV_W - 1) + j
        xc = xc + xp_scr[off:off + tile, :] * cw[j:j + 1, :]
    xp_scr[0:SUBLANES, :] = xp_scr[tile:tile + SUBLANES, :]

    a, u = _lru_gates(xc, wa_ref, ba_ref, wx_ref, bx_ref, lam_ref)
    row = lax.broadcasted_iota(jnp.int32, (tile, 1), 0) % SUBLANES
    a, u = _scan8(a, u, 0, row)
    a_scr[...] = a
    u_scr[...] = u

    def body(gi, hc):
        r0 = pl.multiple_of(gi * SUBLANES, SUBLANES)
        h = a_scr[pl.ds(r0, SUBLANES), :] * hc + u_scr[pl.ds(r0, SUBLANES), :]
        u_scr[pl.ds(r0, SUBLANES), :] = h
        return jnp.broadcast_to(h[SUBLANES - 1:SUBLANES, :], (SUBLANES, LRU_WIDTH))

    hc = lax.fori_loop(0, tile // SUBLANES, body, hc_scr[...])
    hc_scr[...] = hc
    o_ref[...] = (u_scr[...] * _silu(gb_ref[...])).astype(o_ref.dtype)

    @pl.when(t == pl.num_programs(1) - 1)
    def _():
        hlast_ref[...] = hc[0:1, :]


def _lru_weight_specs(layer, nargs):
    def spec(shape):
        idx = (layer,) + (0,) * len(shape)
        if nargs == 1:
            return pl.BlockSpec((None,) + shape, lambda a: idx)
        return pl.BlockSpec((None,) + shape, lambda a, b: idx)
    return [spec((CONV_W, LRU_WIDTH)), spec((1, LRU_WIDTH)),
            spec((LRU_WIDTH, LRU_WIDTH)), spec((1, LRU_WIDTH)),
            spec((LRU_WIDTH, LRU_WIDTH)), spec((1, LRU_WIDTH)), spec((1, LRU_WIDTH))]


def _lru_prompt(z3, lru_w, layer):
    b, s, _ = z3.shape
    tile = min(LRU_TILE, s)
    assert s % tile == 0
    return pl.pallas_call(
        functools.partial(_lru_prompt_kernel, tile=tile),
        grid=(b, s // tile),
        in_specs=[pl.BlockSpec((None, tile, LRU_WIDTH), lambda bi, t: (bi, t, COL_XB // LRU_WIDTH)),
                  pl.BlockSpec((None, tile, LRU_WIDTH), lambda bi, t: (bi, t, COL_GB // LRU_WIDTH))]
        + _lru_weight_specs(layer, 2),
        out_specs=[pl.BlockSpec((None, tile, LRU_WIDTH), lambda bi, t: (bi, t, 0)),
                   pl.BlockSpec((None, 1, LRU_WIDTH), lambda bi, t: (bi, 0, 0))],
        out_shape=[jax.ShapeDtypeStruct((b, s, LRU_WIDTH), BF16),
                   jax.ShapeDtypeStruct((b, 1, LRU_WIDTH), F32)],
        scratch_shapes=[pltpu.VMEM((tile + 2 * SUBLANES, LRU_WIDTH), F32),
                        pltpu.VMEM((tile, LRU_WIDTH), F32),
                        pltpu.VMEM((tile, LRU_WIDTH), F32),
                        pltpu.VMEM((SUBLANES, LRU_WIDTH), F32)],
        compiler_params=_params(("parallel", "arbitrary")),
        name="lru_prompt",
    )(z3, z3, *lru_w)


def _lru_sample_kernel(xb_ref, gb_ref, cs_ref, h0_ref, cw_ref, cb_ref, wa_ref, ba_ref, wx_ref, bx_ref, lam_ref,
                       o_ref, hlast_ref):
    x = xb_ref[...]
    nb, t_len, _ = x.shape
    cs = cs_ref[...]
    row = lax.broadcasted_iota(jnp.int32, (1, t_len, 1), 1)
    cw = cw_ref[...]
    xc = cb_ref[...] + x * cw[CONV_W - 1:CONV_W, :]
    for back in range(1, CONV_W):
        xs = pltpu.roll(x, back, 1)
        for r in range(back):
            src = CONV_W - 1 - back + r
            xs = jnp.where(row == r, cs[:, src:src + 1, :], xs)
        xc = xc + xs * cw[CONV_W - 1 - back:CONV_W - back, :]
    a, u = _lru_gates(xc.reshape(nb * t_len, LRU_WIDTH), wa_ref, ba_ref, wx_ref, bx_ref, lam_ref)
    a = a.reshape(nb, t_len, LRU_WIDTH)
    u = u.reshape(nb, t_len, LRU_WIDTH)
    a, u = _scan8(a, u, 1, row)
    h = a * h0_ref[...] + u
    o_ref[...] = h * _silu(gb_ref[...])
    hlast_ref[...] = h[:, t_len - 1:t_len, :]


def _lru_sample(zs3, state_conv, state_lru4, lru_w, layer):
    b, t_len, _ = zs3.shape
    assert t_len == SUBLANES
    return pl.pallas_call(
        _lru_sample_kernel,
        grid=(1,),
        in_specs=[pl.BlockSpec((b, t_len, LRU_WIDTH), lambda i: (0, 0, COL_XB // LRU_WIDTH)),
                  pl.BlockSpec((b, t_len, LRU_WIDTH), lambda i: (0, 0, COL_GB // LRU_WIDTH)),
                  pl.BlockSpec((None, b, CONV_W - 1, LRU_WIDTH), lambda i: (layer, 0, 0, 0)),
                  pl.BlockSpec((None, b, 1, LRU_WIDTH), lambda i: (layer, 0, 0, 0))]
        + _lru_weight_specs(layer, 1),
        out_specs=[pl.BlockSpec((b, t_len, LRU_WIDTH), lambda i: (0, 0, 0)),
                   pl.BlockSpec((b, 1, LRU_WIDTH), lambda i: (0, 0, 0))],
        out_shape=[jax.ShapeDtypeStruct((b, t_len, LRU_WIDTH), F32),
                   jax.ShapeDtypeStruct((b, 1, LRU_WIDTH), F32)],
        compiler_params=_params(("arbitrary",)),
        name="lru_sample",
    )(zs3, zs3, state_conv, state_lru4, *lru_w)


def _split3(x):
    hi = x.astype(BF16)
    r = x - hi.astype(F32)
    mid = r.astype(BF16)
    lo = (r - mid.astype(F32)).astype(BF16)
    return hi, mid, lo


def _cumsum_rows(x):
    n = x.shape[0]
    row = lax.broadcasted_iota(jnp.int32, (n, 1), 0)
    if n == SUBLANES:
        for sh in (1, 2, 4):
            x = x + jnp.where(row >= sh, pltpu.roll(x, sh, 0), 0.0)
        return x
    col = lax.broadcasted_iota(jnp.int32, (1, n), 1)
    tri = (row >= col).astype(BF16)
    out = None
    for part in _split3(x):
        term = jnp.dot(tri, part, preferred_element_type=F32)
        out = term if out is None else out + term
    return out


def _pad_rows(x, n):
    if x.shape[0] == n:
        return x
    return jnp.concatenate([x, jnp.zeros((n - x.shape[0], x.shape[1]), x.dtype)], axis=0)


def _lower_bound(logits, layer):
    e = jnp.exp(logits - jnp.max(logits, axis=0, keepdims=True))
    den = jnp.sum(e, axis=0, keepdims=True)
    num = jnp.zeros_like(den)
    for r in range(1, layer + 1):
        num = num + e[r:r + 1, :]
    return num / den


def _hgrn_chunk(qc2, fc2, v, gc, st, lb2, cg):
    n = v.shape[0]
    nblk = n // SUBLANES
    lane = lax.broadcasted_iota(jnp.int32, (1, LANES), 1)
    head_a = lane < C_VDIM
    sub = lax.broadcasted_iota(jnp.int32, (1, SUBLANES, 1), 1)
    nt = (((1,), (1,)), ((), ()))
    blocked = lambda x: x.reshape(nblk, SUBLANES, LANES)
    st_b = st.astype(BF16)
    v_b = v.astype(BF16)
    v3 = blocked(v)
    outs, decays, k_ends = [], [], []
    for a in range(2):
        qc = qc2[:, a * C_KDIM:(a + 1) * C_KDIM]
        fc = fc2[:, a * C_KDIM:(a + 1) * C_KDIM]
        lb = lb2[:, a * C_KDIM:(a + 1) * C_KDIM]
        q = _silu(qc) * (C_KDIM ** -0.5)
        g = lb + (1.0 - lb) * jax.nn.sigmoid(fc)
        k = 1.0 - g
        b = _cumsum_rows(jnp.log(g) * LOG2_E)
        o = lax.dot_general((q * jnp.exp2(b)).astype(BF16), st_b, nt, preferred_element_type=F32)
        q3, k3, b3 = blocked(q), blocked(k), blocked(b)
        o3 = jnp.sum(q3 * k3, axis=-1, keepdims=True) * v3
        for d in range(1, SUBLANES):
            w = q3 * pltpu.roll(k3, d, 1) * jnp.exp2(jnp.minimum(b3 - pltpu.roll(b3, d, 1), 0.0))
            a_d = jnp.where(sub >= d, jnp.sum(w, axis=-1, keepdims=True), 0.0)
            o3 = o3 + a_d * pltpu.roll(v3, d, 1)
        o = o + o3.reshape(n, LANES)
        if nblk > 1:
            blocks = [jnp.zeros((SUBLANES, n), F32)]
            for i in range(1, nblk):
                lo = i * SUBLANES
                ref = b[lo - 1:lo, :]
                kt = _pad_rows(k[:lo] * jnp.exp2(jnp.minimum(ref - b[:lo], 0.0)), n).astype(BF16)
                qt = (q[lo:lo + SUBLANES] * jnp.exp2(b[lo:lo + SUBLANES] - ref)).astype(BF16)
                blocks.append(lax.dot_general(qt, kt, nt, preferred_element_type=F32))
            attn = jnp.concatenate(blocks, axis=0).astype(BF16)
            o = o + jnp.dot(attn, v_b, preferred_element_type=F32)
        outs.append(o)
        last = b[n - 1:n, :]
        decays.append(jnp.exp2(last))
        k_ends.append(k * jnp.exp2(last - b))
    o = jnp.where(head_a, outs[0], outs[1])
    v_t = _pad_rows(v, LANES).T.astype(BF16)
    top = lax.broadcasted_iota(jnp.int32, (LANES, 1), 0) < C_VDIM
    new_a = st * decays[0] + jnp.dot(v_t, _pad_rows(k_ends[0], LANES).astype(BF16), preferred_element_type=F32)
    new_b = st * decays[1] + jnp.dot(v_t, _pad_rows(k_ends[1], LANES).astype(BF16), preferred_element_type=F32)
    st_new = jnp.where(top, new_a, new_b)
    sq = o * o
    ms_a = jnp.sum(jnp.where(head_a, sq, 0.0), axis=-1, keepdims=True) * (1.0 / C_VDIM)
    ms_b = jnp.sum(jnp.where(head_a, 0.0, sq), axis=-1, keepdims=True) * (1.0 / C_VDIM)
    inv = jnp.where(head_a, lax.rsqrt(ms_a + RMS_EPS), lax.rsqrt(ms_b + RMS_EPS))
    return o * inv * cg * _silu(gc), st_new


def _hgrn_prompt_kernel(qc_ref, fc_ref, ic_ref, gc_ref, lbl_ref, cg_ref, o_ref, st_ref, st_scr, *, layer, tile):
    t = pl.program_id(2)

    @pl.when(t == 0)
    def _():
        st_scr[...] = jnp.zeros_like(st_scr)

    lb2 = _lower_bound(lbl_ref[...], layer)
    cg = cg_ref[...]

    def body(c, carry):
        r0 = pl.multiple_of(c * HGRN_CHUNK, HGRN_CHUNK)
        rows = pl.ds(r0, HGRN_CHUNK)
        out, st_new = _hgrn_chunk(qc_ref[rows, :], fc_ref[rows, :], ic_ref[rows, :], gc_ref[rows, :],
                                  st_scr[...], lb2, cg)
        o_ref[rows, :] = out.astype(o_ref.dtype)
        st_scr[...] = st_new
        return carry

    lax.fori_loop(0, tile // HGRN_CHUNK, body, 0, unroll=2)

    @pl.when(t == pl.num_programs(2) - 1)
    def _():
        st_ref[...] = st_scr[...]


def _hgrn_prompt(z3, lb_logits, cnorm3, layer):
    b, s, _ = z3.shape
    tile = min(HGRN_TILE, s)
    assert s % tile == 0 and tile % HGRN_CHUNK == 0
    kw, vw = 2 * C_KDIM, 2 * C_VDIM
    return pl.pallas_call(
        functools.partial(_hgrn_prompt_kernel, layer=layer, tile=tile),
        grid=(b, HEAD_PAIRS, s // tile),
        in_specs=[pl.BlockSpec((None, tile, kw), lambda bi, p, t: (bi, t, COL_QC // kw + p)),
                  pl.BlockSpec((None, tile, kw), lambda bi, p, t: (bi, t, COL_FC // kw + p)),
                  pl.BlockSpec((None, tile, vw), lambda bi, p, t: (bi, t, COL_IC // vw + p)),
                  pl.BlockSpec((None, tile, vw), lambda bi, p, t: (bi, t, COL_GC // vw + p)),
                  pl.BlockSpec((DEPTH, kw), lambda bi, p, t: (0, p)),
                  pl.BlockSpec((None, 1, vw), lambda bi, p, t: (layer, 0, p))],
        out_specs=[pl.BlockSpec((None, tile, vw), lambda bi, p, t: (bi, t, p)),
                   pl.BlockSpec((None, None, LANES, LANES), lambda bi, p, t: (bi, p, 0, 0))],
        out_shape=[jax.ShapeDtypeStruct((b, s, C_VWIDTH), BF16),
                   jax.ShapeDtypeStruct((b, HEAD_PAIRS, LANES, LANES), F32)],
        scratch_shapes=[pltpu.VMEM((LANES, LANES), F32)],
        compiler_params=_params(("parallel", "parallel", "arbitrary")),
        name="hgrn_prompt",
    )(z3, z3, z3, z3, lb_logits, cnorm3)


def _hgrn_sample_kernel(qc_ref, fc_ref, ic_ref, gc_ref, st_in_ref, lbl_ref, cg_ref, o_ref, st_ref, *, layer):
    lb = _lower_bound(lbl_ref[...], layer)
    kw, vw = 2 * C_KDIM, 2 * C_VDIM
    for p in range(HEAD_PAIRS):
        ks, vs = slice(p * kw, (p + 1) * kw), slice(p * vw, (p + 1) * vw)
        out, st_new = _hgrn_chunk(qc_ref[:, ks], fc_ref[:, ks], ic_ref[:, vs], gc_ref[:, vs], st_in_ref[p],
                                  lb[:, ks], cg_ref[:, vs])
        o_ref[:, vs] = out
        st_ref[p] = st_new


def _hgrn_sample(zs3, state_t, lb_logits, cnorm3, layer):
    b, t_len, _ = zs3.shape
    assert t_len == SUBLANES
    return pl.pallas_call(
        functools.partial(_hgrn_sample_kernel, layer=layer),
        grid=(b,),
        in_specs=[pl.BlockSpec((None, t_len, C_KWIDTH), lambda bi: (bi, 0, COL_QC // C_KWIDTH)),
                  pl.BlockSpec((None, t_len, C_KWIDTH), lambda bi: (bi, 0, COL_FC // C_KWIDTH)),
                  pl.BlockSpec((None, t_len, C_VWIDTH), lambda bi: (bi, 0, COL_IC // C_VWIDTH)),
                  pl.BlockSpec((None, t_len, C_VWIDTH), lambda bi: (bi, 0, COL_GC // C_VWIDTH)),
                  pl.BlockSpec((None, None, HEAD_PAIRS, LANES, LANES), lambda bi: (layer, bi, 0, 0, 0)),
                  pl.BlockSpec((DEPTH, C_KWIDTH), lambda bi: (0, 0)),
                  pl.BlockSpec((None, 1, C_VWIDTH), lambda bi: (layer, 0, 0))],
        out_specs=[pl.BlockSpec((None, t_len, C_VWIDTH), lambda bi: (bi, 0, 0)),
                   pl.BlockSpec((None, HEAD_PAIRS, LANES, LANES), lambda bi: (bi, 0, 0, 0))],
        out_shape=[jax.ShapeDtypeStruct((b, t_len, C_VWIDTH), F32),
                   jax.ShapeDtypeStruct((b, HEAD_PAIRS, LANES, LANES), F32)],
        compiler_params=_params(("parallel",)),
        name="hgrn_sample",
    )(zs3, zs3, zs3, zs3, state_t, lb_logits, cnorm3)


def _block_diag(w):
    depth, nb, d, _ = w.shape
    eye = jnp.eye(nb, dtype=w.dtype)
    full = jnp.einsum("lhij,hg->lhigj", w, eye)
    return full.reshape(depth, nb * d, nb * d).astype(BF16)


def _state_to_pairs(st):
    lead = st.shape[:-3]
    t = jnp.swapaxes(st, -1, -2)
    return t.reshape(lead + (HEAD_PAIRS, 2 * C_VDIM, C_KDIM))


def _pairs_to_state(sp):
    lead = sp.shape[:-3]
    t = sp.reshape(lead + (C_HEADS, C_VDIM, C_KDIM))
    return jnp.swapaxes(t, -1, -2)


def kernel(x_prompt, x_sample, cache_k_win, cache_v_win, state_conv, state_lru, state_hgrn, norm_g, w_in, conv_w,
           conv_b, lru_w_a, lru_b_a, lru_w_x, lru_b_x, lru_lambda, hgrn_lb_logits, hgrn_norm_g, w_out, final_norm_g):
    bp, sp, _ = x_prompt.shape
    bs, ts, _ = x_sample.shape
    w_buf = cache_k_win.shape[2]
    assert w_buf == MAX_WINDOW and sp <= MAX_WINDOW

    w_in_bf = w_in.astype(BF16)
    w_out_bf = w_out.astype(BF16)
    norm_g3 = norm_g.reshape(DEPTH, 1, D_MODEL)
    final_g2 = final_norm_g.reshape(1, D_MODEL)
    row3 = lambda p: p.reshape(DEPTH, 1, -1)
    lru_w = (conv_w, row3(conv_b), _block_diag(lru_w_a), row3(lru_b_a), _block_diag(lru_w_x), row3(lru_b_x),
             row3(lru_lambda))
    cnorm3 = row3(hgrn_norm_g)
    cache_kt = jnp.transpose(cache_k_win, (0, 1, 3, 4, 2))
    cache_vt = jnp.transpose(cache_v_win, (0, 1, 3, 4, 2))
    state_lru4 = state_lru.reshape(DEPTH, bs, 1, LRU_WIDTH)
    state_t = _state_to_pairs(state_hgrn)
    bias_p = _prompt_bias()
    bias_s = _sample_bias(ts)

    xp = x_prompt.reshape(bp * sp, D_MODEL)
    xs = x_sample.reshape(bs * ts, D_MODEL)
    outs = {k: [] for k in ("kp", "vp", "ks", "vs", "cp", "cs", "lp", "ls", "hp", "hs")}
    for layer in range(DEPTH):
        last = layer == DEPTH - 1
        z = _inproj(xp, norm_g3, w_in_bf, layer)
        z3 = z.reshape(bp, sp, IN_WIDTH)
        mix_a = _attn_prompt(z3, bias_p)
        mix_b, h_last = _lru_prompt(z3, lru_w, layer)
        mix_c, st_p = _hgrn_prompt(z3, hgrn_lb_logits, cnorm3, layer)
        xp = _outproj(xp, mix_a.reshape(bp * sp, -1), mix_b.reshape(bp * sp, -1), mix_c.reshape(bp * sp, -1),
                      w_out_bf, final_g2, layer, last)
        outs["kp"].append(z3[:, :, COL_KA:COL_KA + A_WIDTH].reshape(bp, sp, A_HEADS, HEAD_DIM))
        outs["vp"].append(z3[:, :, COL_VA:COL_VA + A_WIDTH].reshape(bp, sp, A_HEADS, HEAD_DIM))
        outs["cp"].append(z3[:, sp - (CONV_W - 1):, COL_XB:COL_XB + LRU_WIDTH])
        outs["lp"].append(h_last.reshape(bp, LRU_WIDTH))
        outs["hp"].append(_pairs_to_state(st_p))
        zs = _inproj(xs, norm_g3, w_in_bf, layer)
        zs3 = zs.reshape(bs, ts, IN_WIDTH)
        smix_a = _attn_sample(zs3, cache_kt, cache_vt, bias_s, layer)
        smix_b, sh_last = _lru_sample(zs3, state_conv, state_lru4, lru_w, layer)
        smix_c, st_s = _hgrn_sample(zs3, state_t, hgrn_lb_logits, cnorm3, layer)
        xs = _outproj(xs, smix_a.reshape(bs * ts, -1), smix_b.reshape(bs * ts, -1), smix_c.reshape(bs * ts, -1),
                      w_out_bf, final_g2, layer, last)
        outs["ks"].append(zs3[:, :, COL_KA:COL_KA + A_WIDTH].reshape(bs, ts, A_HEADS, HEAD_DIM))
        outs["vs"].append(zs3[:, :, COL_VA:COL_VA + A_WIDTH].reshape(bs, ts, A_HEADS, HEAD_DIM))
        outs["cs"].append(zs3[:, ts - (CONV_W - 1):, COL_XB:COL_XB + LRU_WIDTH])
        outs["ls"].append(sh_last.reshape(bs, LRU_WIDTH))
        outs["hs"].append(_pairs_to_state(st_s))
    st = lambda k: jnp.stack(outs[k])
    return (xp.reshape(bp, sp, D_MODEL), xs.reshape(bs, ts, D_MODEL), st("kp"), st("vp"), st("ks"), st("vs"),
            st("cp"), st("cs"), st("lp"), st("ls"), st("hp"), st("hs"))
```

```python
import functools

import jax
import jax.numpy as jnp
import numpy as np
from jax import lax
from jax.experimental import pallas as pl
from jax.experimental.pallas import tpu as pltpu

F32 = jnp.float32
BF16 = jnp.bfloat16

D_MODEL = 1024
DEPTH = 4
HEAD_DIM = 64
A_HEADS = 6
A_WIDTH = A_HEADS * HEAD_DIM
DILATED = ((128, 1), (512, 4), (2048, 16))
MAX_WINDOW = 2048
LRU_BLOCKS = 6
LRU_BLOCK_DIM = 64
LRU_WIDTH = LRU_BLOCKS * LRU_BLOCK_DIM
CONV_W = 4
LRU_C = 8.0
C_HEADS = 6
C_KDIM = 128
C_VDIM = 64
C_KWIDTH = C_HEADS * C_KDIM
C_VWIDTH = C_HEADS * C_VDIM
MIX_WIDTH = A_WIDTH + LRU_WIDTH + C_VWIDTH
IN_WIDTH = 4 * A_WIDTH + 2 * LRU_WIDTH + 2 * C_KWIDTH + 2 * C_VWIDTH
RMS_EPS = 1e-6
LOG2_E = 1.4426950408889634

LANES = 128
SUBLANES = 8
HEAD_PAIRS = A_HEADS // 2
VMEM_LIMIT = 56 * 1024 * 1024

COL_QA, COL_KA, COL_VA, COL_GA = 0, A_WIDTH, 2 * A_WIDTH, 3 * A_WIDTH
COL_XB = 4 * A_WIDTH
COL_GB = COL_XB + LRU_WIDTH
COL_QC = COL_GB + LRU_WIDTH
COL_FC = COL_QC + C_KWIDTH
COL_IC = COL_FC + C_KWIDTH
COL_GC = COL_IC + C_VWIDTH

ATT_TQ = 256
ATT_TK = 512
HGRN_CHUNK = 64
HGRN_GROUP = 256
HGRN_TILE = 512
LRU_TILE = 512
PROJ_TM = 256


def _silu(x):
    return x * jax.nn.sigmoid(x)


def _params(sem, **kw):
    return pltpu.CompilerParams(dimension_semantics=sem, vmem_limit_bytes=VMEM_LIMIT, **kw)


def _inproj_kernel(x_ref, g_ref, w_ref, z_ref, *, n_chunk):
    x = x_ref[...]
    h = (x * lax.rsqrt(jnp.mean(x * x, axis=-1, keepdims=True) + RMS_EPS) * g_ref[...]).astype(BF16)
    for n in range(IN_WIDTH // n_chunk):
        sl = slice(n * n_chunk, (n + 1) * n_chunk)
        z_ref[:, sl] = jnp.dot(h, w_ref[:, sl], preferred_element_type=F32)


def _inproj(x2d, norm_g3, w_in_bf, layer):
    m = x2d.shape[0]
    tm = min(PROJ_TM, m)
    assert m % tm == 0
    return pl.pallas_call(
        functools.partial(_inproj_kernel, n_chunk=512),
        grid=(m // tm,),
        in_specs=[
            pl.BlockSpec((tm, D_MODEL), lambda i: (i, 0)),
            pl.BlockSpec((None, 1, D_MODEL), lambda i: (layer, 0, 0)),
            pl.BlockSpec((None, D_MODEL, IN_WIDTH), lambda i: (layer, 0, 0)),
        ],
        out_specs=pl.BlockSpec((tm, IN_WIDTH), lambda i: (i, 0)),
        out_shape=jax.ShapeDtypeStruct((m, IN_WIDTH), F32),
        compiler_params=_params(("parallel",)),
        name="inproj",
    )(x2d, norm_g3, w_in_bf)


def _outproj_kernel(x_ref, ma_ref, mb_ref, mc_ref, w_ref, g_ref, y_ref, *, final_norm):
    y = x_ref[...]
    y = y + jnp.dot(ma_ref[...].astype(BF16), w_ref[0:A_WIDTH, :], preferred_element_type=F32)
    y = y + jnp.dot(mb_ref[...].astype(BF16), w_ref[A_WIDTH:A_WIDTH + LRU_WIDTH, :], preferred_element_type=F32)
    y = y + jnp.dot(mc_ref[...].astype(BF16), w_ref[A_WIDTH + LRU_WIDTH:, :], preferred_element_type=F32)
    if final_norm:
        y = y * lax.rsqrt(jnp.mean(y * y, axis=-1, keepdims=True) + RMS_EPS) * g_ref[...]
    y_ref[...] = y


def _outproj(x2d, mix_a, mix_b, mix_c, w_out_bf, final_g2, layer, final_norm):
    m = x2d.shape[0]
    tm = min(PROJ_TM, m)
    assert m % tm == 0
    mspec = lambda w: pl.BlockSpec((tm, w), lambda i: (i, 0))
    return pl.pallas_call(
        functools.partial(_outproj_kernel, final_norm=final_norm),
        grid=(m // tm,),
        in_specs=[
            mspec(D_MODEL), mspec(A_WIDTH), mspec(LRU_WIDTH), mspec(C_VWIDTH),
            pl.BlockSpec((None, MIX_WIDTH, D_MODEL), lambda i: (layer, 0, 0)),
            pl.BlockSpec((1, D_MODEL), lambda i: (0, 0)),
        ],
        out_specs=mspec(D_MODEL),
        out_shape=jax.ShapeDtypeStruct((m, D_MODEL), F32),
        compiler_params=_params(("parallel",)),
        name="outproj",
    )(x2d, mix_a, mix_b, mix_c, w_out_bf, final_g2)


def _branch_count(delta):
    delta = np.asarray(delta)
    cnt = np.zeros(delta.shape, np.int32)
    for window, dil in DILATED:
        cnt += ((delta >= 0) & (delta <= window) & (delta % dil == 0)).astype(np.int32)
    return cnt


def _log_count(cnt):
    with np.errstate(divide="ignore"):
        return np.where(cnt > 0, np.log(np.maximum(cnt, 1).astype(np.float64)), -np.inf).astype(np.float32)


ATT_FAR_OFF = (DILATED[1][0] + ATT_TK - 1) // ATT_TQ + 1


def _prompt_bias():
    qi = np.arange(ATT_TQ)[None, :]
    ki = np.arange(ATT_TK)[:, None]
    tabs = []
    for off in range(ATT_FAR_OFF + 1):
        b = _log_count(_branch_count(off * ATT_TQ + qi - ki)) * np.float32(LOG2_E)
        tabs.append(np.concatenate([b, b], axis=1))
    return jnp.asarray(np.stack(tabs))


def _attn_prompt_kernel(q_ref, k_ref, v_ref, g_ref, bias_ref, kt_in_ref, vt_in_ref, o_ref, kt_ref, vt_ref):
    i = pl.program_id(2)
    lane = lax.broadcasted_iota(jnp.int32, (1, LANES), 1)
    head_a = lane < HEAD_DIM
    q = q_ref[...] * (HEAD_DIM ** -0.5 * LOG2_E)
    q2 = jnp.concatenate([jnp.where(head_a, q, 0.0), jnp.where(head_a, 0.0, q)], axis=0).astype(BF16)

    n_blocks = (i * ATT_TQ) // ATT_TK + 1

    def body(j, carry):
        m, l, acc = carry
        k0 = pl.multiple_of(j * ATT_TK, ATT_TK)
        kb = k_ref[pl.ds(k0, ATT_TK), :].astype(BF16)
        vb = v_ref[pl.ds(k0, ATT_TK), :].astype(BF16)
        s = lax.dot_general(kb, q2, (((1,), (1,)), ((), ())), preferred_element_type=F32)
        s = s + bias_ref[jnp.minimum(i - j * (ATT_TK // ATT_TQ), ATT_FAR_OFF)]
        m_new = jnp.maximum(m, jnp.max(s, axis=0, keepdims=True))
        alpha = jnp.exp2(m - m_new)
        p = jnp.exp2(s - m_new)
        l = alpha * l + jnp.sum(p, axis=0, keepdims=True)
        pv = lax.dot_general(vb, p.astype(BF16), (((0,), (0,)), ((), ())), preferred_element_type=F32)
        return m_new, l, alpha * acc + pv

    init = (jnp.full((1, 2 * ATT_TQ), -jnp.inf, F32), jnp.zeros((1, 2 * ATT_TQ), F32),
            jnp.zeros((LANES, 2 * ATT_TQ), F32))
    _, l, acc = lax.fori_loop(0, n_blocks, body, init)
    o2 = (acc / l).T
    o = jnp.where(head_a, o2[:ATT_TQ], o2[ATT_TQ:])
    o_ref[...] = (o * _silu(g_ref[...])).astype(o_ref.dtype)

    @pl.when(i == 0)
    def _():
        s_len = k_ref.shape[0]
        for src, dst in ((k_ref, kt_ref), (v_ref, vt_ref)):
            for c in range(s_len // LANES):
                blk = src[c * LANES:(c + 1) * LANES, :].T
                dst[:, :, c * LANES:(c + 1) * LANES] = blk.reshape(2, HEAD_DIM, LANES)


def _attn_prompt(z3, bias, kt_all, vt_all, layer):
    b, s, _ = z3.shape
    assert ATT_TK % ATT_TQ == 0 and s % ATT_TK == 0 and s <= MAX_WINDOW and ATT_TQ % DILATED[2][1] == 0
    col = lambda base: (lambda bi, p, i: (bi, 0, base // LANES + p))
    win = pl.BlockSpec((None, None, 2, HEAD_DIM, s), lambda bi, p, i: (layer, bi, p, 0, 0))
    return pl.pallas_call(
        _attn_prompt_kernel,
        grid=(b, HEAD_PAIRS, s // ATT_TQ),
        in_specs=[
            pl.BlockSpec((None, ATT_TQ, LANES), lambda bi, p, i: (bi, i, COL_QA // LANES + p)),
            pl.BlockSpec((None, s, LANES), col(COL_KA)),
            pl.BlockSpec((None, s, LANES), col(COL_VA)),
            pl.BlockSpec((None, ATT_TQ, LANES), lambda bi, p, i: (bi, i, COL_GA // LANES + p)),
            pl.BlockSpec((ATT_FAR_OFF + 1, ATT_TK, 2 * ATT_TQ), lambda bi, p, i: (0, 0, 0)),
            pl.BlockSpec(memory_space=pl.ANY),
            pl.BlockSpec(memory_space=pl.ANY),
        ],
        out_specs=[pl.BlockSpec((None, ATT_TQ, LANES), lambda bi, p, i: (bi, i, p)), win, win],
        out_shape=[jax.ShapeDtypeStruct((b, s, A_WIDTH), BF16),
                   jax.ShapeDtypeStruct(kt_all.shape, F32), jax.ShapeDtypeStruct(vt_all.shape, F32)],
        input_output_aliases={5: 1, 6: 2},
        compiler_params=_params(("parallel", "parallel", "arbitrary")),
        name="attn_prompt",
    )(z3, z3, z3, z3, bias, kt_all, vt_all)


def _sample_bias(t_len):
    qpos = MAX_WINDOW + np.arange(t_len)[:, None]
    bias_c = _log_count(_branch_count(qpos - np.arange(MAX_WINDOW)[None, :]))
    bias_n = _log_count(_branch_count(qpos - (MAX_WINDOW + np.arange(t_len)[None, :])))
    return jnp.asarray(bias_c), jnp.asarray(bias_n)


def _split_heads(x):
    return jnp.stack([x[:, h * HEAD_DIM:(h + 1) * HEAD_DIM] for h in range(A_HEADS)], axis=0)


def _attn_sample_kernel(q_ref, k_ref, v_ref, g_ref, kt_ref, vt_ref, bias_c_ref, bias_n_ref, o_ref):
    q3 = _split_heads(q_ref[...] * (HEAD_DIM ** -0.5))
    kn3 = _split_heads(k_ref[...])
    vn3 = _split_heads(v_ref[...])
    bdot = lambda a, b, ca, cb: lax.dot_general(a, b, (((ca,), (cb,)), ((0,), (0,))), preferred_element_type=F32)
    s_c = bdot(q3, kt_ref[...], 2, 1) + bias_c_ref[...][None]
    s_n = bdot(q3, kn3, 2, 2) + bias_n_ref[...][None]
    m = jnp.maximum(jnp.max(s_c, axis=-1, keepdims=True), jnp.max(s_n, axis=-1, keepdims=True))
    p_c = jnp.exp(s_c - m)
    p_n = jnp.exp(s_n - m)
    l = jnp.sum(p_c, axis=-1, keepdims=True) + jnp.sum(p_n, axis=-1, keepdims=True)
    acc = bdot(p_c, vt_ref[...], 2, 2) + bdot(p_n, vn3, 2, 1)
    o3 = acc / l
    o = jnp.concatenate([o3[h] for h in range(A_HEADS)], axis=-1)
    o_ref[...] = o * _silu(g_ref[...])


def _attn_sample(zs3, cache_kt, cache_vt, bias, layer):
    b, t_len, _ = zs3.shape
    w_buf = cache_kt.shape[-1]
    zcol = lambda base: pl.BlockSpec((None, t_len, A_WIDTH), lambda bi: (bi, 0, base // A_WIDTH))
    cache = pl.BlockSpec((None, None, A_HEADS, HEAD_DIM, w_buf), lambda bi: (layer, bi, 0, 0, 0))
    bias_c, bias_n = bias
    return pl.pallas_call(
        _attn_sample_kernel,
        grid=(b,),
        in_specs=[zcol(COL_QA), zcol(COL_KA), zcol(COL_VA), zcol(COL_GA), cache, cache,
                  pl.BlockSpec(bias_c.shape, lambda bi: (0, 0)), pl.BlockSpec(bias_n.shape, lambda bi: (0, 0))],
        out_specs=pl.BlockSpec((None, t_len, A_WIDTH), lambda bi: (bi, 0, 0)),
        out_shape=jax.ShapeDtypeStruct((b, t_len, A_WIDTH), F32),
        compiler_params=_params(("parallel",)),
        name="attn_sample",
    )(zs3, zs3, zs3, zs3, cache_kt, cache_vt, bias_c, bias_n)


def _neg_expm1(x):
    series = -x * (1.0 + x * (0.5 + x * (1.0 / 6.0 + x * (1.0 / 24.0 + x * (1.0 / 120.0)))))
    return jnp.where(x > -0.1, series, 1.0 - jnp.exp(x))


def _softplus(y):
    return jnp.maximum(y, 0.0) + jnp.log1p(jnp.exp(-jnp.abs(y)))


def _lru_gates(xc, wa_ref, ba_ref, wx_ref, bx_ref, lam_ref):
    xcb = xc.astype(BF16)
    r = jax.nn.sigmoid(jnp.dot(xcb, wa_ref[...], preferred_element_type=F32) + ba_ref[...])
    i = jax.nn.sigmoid(jnp.dot(xcb, wx_ref[...], preferred_element_type=F32) + bx_ref[...])
    log_a = -LRU_C * r * _softplus(-lam_ref[...])
    a = jnp.exp(log_a)
    u = jnp.sqrt(_neg_expm1(2.0 * log_a)) * (i * xc)
    return a, u


def _scan8(a, u, axis, row):
    for sh in (1, 2, 4):
        keep = row >= sh
        a_prev = jnp.where(keep, pltpu.roll(a, sh, axis), 1.0)
        u_prev = jnp.where(keep, pltpu.roll(u, sh, axis), 0.0)
        u = a * u_prev + u
        a = a * a_prev
    return a, u


def _lru_prompt_kernel(xb_ref, gb_ref, cw_ref, cb_ref, wa_ref, ba_ref, wx_ref, bx_ref, lam_ref,
                       o_ref, hlast_ref, xp_scr, a_scr, u_scr, hc_scr, *, tile):
    t = pl.program_id(1)

    @pl.when(t == 0)
    def _():
        xp_scr[0:SUBLANES, :] = jnp.zeros((SUBLANES, LRU_WIDTH), F32)
        hc_scr[...] = jnp.zeros_like(hc_scr)

    xp_scr[SUBLANES:SUBLANES + tile, :] = xb_ref[...]
    cw = cw_ref[...]
    xc = cb_ref[...]
    for j in range(CONV_W):
        off = SUBLANES - (CONV_W - 1) + j
        xc = xc + xp_scr[off:off + tile, :] * cw[j:j + 1, :]
    xp_scr[0:SUBLANES, :] = xp_scr[tile:tile + SUBLANES, :]

    a, u = _lru_gates(xc, wa_ref, ba_ref, wx_ref, bx_ref, lam_ref)
    row = lax.broadcasted_iota(jnp.int32, (tile, 1), 0) % SUBLANES
    a, u = _scan8(a, u, 0, row)
    a_scr[...] = a
    u_scr[...] = u

    def body(gi, hc):
        r0 = pl.multiple_of(gi * SUBLANES, SUBLANES)
        h = a_scr[pl.ds(r0, SUBLANES), :] * hc + u_scr[pl.ds(r0, SUBLANES), :]
        u_scr[pl.ds(r0, SUBLANES), :] = h
        return jnp.broadcast_to(h[SUBLANES - 1:SUBLANES, :], (SUBLANES, LRU_WIDTH))

    hc = lax.fori_loop(0, tile // SUBLANES, body, hc_scr[...])
    hc_scr[...] = hc
    o_ref[...] = (u_scr[...] * _silu(gb_ref[...])).astype(o_ref.dtype)

    @pl.when(t == pl.num_programs(1) - 1)
    def _():
        hlast_ref[...] = hc[0:1, :]


def _lru_weight_specs(layer, nargs):
    def spec(shape):
        idx = (layer,) + (0,) * len(shape)
        if nargs == 1:
            return pl.BlockSpec((None,) + shape, lambda a: idx)
        return pl.BlockSpec((None,) + shape, lambda a, b: idx)
    return [spec((CONV_W, LRU_WIDTH)), spec((1, LRU_WIDTH)),
            spec((LRU_WIDTH, LRU_WIDTH)), spec((1, LRU_WIDTH)),
            spec((LRU_WIDTH, LRU_WIDTH)), spec((1, LRU_WIDTH)), spec((1, LRU_WIDTH))]


def _lru_prompt(z3, lru_w, layer):
    b, s, _ = z3.shape
    tile = min(LRU_TILE, s)
    assert s % tile == 0
    return pl.pallas_call(
        functools.partial(_lru_prompt_kernel, tile=tile),
        grid=(b, s // tile),
        in_specs=[pl.BlockSpec((None, tile, LRU_WIDTH), lambda bi, t: (bi, t, COL_XB // LRU_WIDTH)),
                  pl.BlockSpec((None, tile, LRU_WIDTH), lambda bi, t: (bi, t, COL_GB // LRU_WIDTH))]
        + _lru_weight_specs(layer, 2),
        out_specs=[pl.BlockSpec((None, tile, LRU_WIDTH), lambda bi, t: (bi, t, 0)),
                   pl.BlockSpec((None, 1, LRU_WIDTH), lambda bi, t: (bi, 0, 0))],
        out_shape=[jax.ShapeDtypeStruct((b, s, LRU_WIDTH), BF16),
                   jax.ShapeDtypeStruct((b, 1, LRU_WIDTH), F32)],
        scratch_shapes=[pltpu.VMEM((tile + 2 * SUBLANES, LRU_WIDTH), F32),
                        pltpu.VMEM((tile, LRU_WIDTH), F32),
                        pltpu.VMEM((tile, LRU_WIDTH), F32),
                        pltpu.VMEM((SUBLANES, LRU_WIDTH), F32)],
        compiler_params=_params(("parallel", "arbitrary")),
        name="lru_prompt",
    )(z3, z3, *lru_w)


def _lru_sample_kernel(xb_ref, gb_ref, cs_ref, h0_ref, cw_ref, cb_ref, wa_ref, ba_ref, wx_ref, bx_ref, lam_ref,
                       o_ref, hlast_ref):
    x = xb_ref[...]
    nb, t_len, _ = x.shape
    cs = cs_ref[...]
    row = lax.broadcasted_iota(jnp.int32, (1, t_len, 1), 1)
    cw = cw_ref[...]
    xc = cb_ref[...] + x * cw[CONV_W - 1:CONV_W, :]
    for back in range(1, CONV_W):
        xs = pltpu.roll(x, back, 1)
        for r in range(back):
            src = CONV_W - 1 - back + r
            xs = jnp.where(row == r, cs[:, src:src + 1, :], xs)
        xc = xc + xs * cw[CONV_W - 1 - back:CONV_W - back, :]
    a, u = _lru_gates(xc.reshape(nb * t_len, LRU_WIDTH), wa_ref, ba_ref, wx_ref, bx_ref, lam_ref)
    a = a.reshape(nb, t_len, LRU_WIDTH)
    u = u.reshape(nb, t_len, LRU_WIDTH)
    a, u = _scan8(a, u, 1, row)
    h = a * h0_ref[...] + u
    o_ref[...] = h * _silu(gb_ref[...])
    hlast_ref[...] = h[:, t_len - 1:t_len, :]


def _lru_sample(zs3, state_conv, state_lru4, lru_w, layer):
    b, t_len, _ = zs3.shape
    assert t_len == SUBLANES
    return pl.pallas_call(
        _lru_sample_kernel,
        grid=(1,),
        in_specs=[pl.BlockSpec((b, t_len, LRU_WIDTH), lambda i: (0, 0, COL_XB // LRU_WIDTH)),
                  pl.BlockSpec((b, t_len, LRU_WIDTH), lambda i: (0, 0, COL_GB // LRU_WIDTH)),
                  pl.BlockSpec((None, b, CONV_W - 1, LRU_WIDTH), lambda i: (layer, 0, 0, 0)),
                  pl.BlockSpec((None, b, 1, LRU_WIDTH), lambda i: (layer, 0, 0, 0))]
        + _lru_weight_specs(layer, 1),
        out_specs=[pl.BlockSpec((b, t_len, LRU_WIDTH), lambda i: (0, 0, 0)),
                   pl.BlockSpec((b, 1, LRU_WIDTH), lambda i: (0, 0, 0))],
        out_shape=[jax.ShapeDtypeStruct((b, t_len, LRU_WIDTH), F32),
                   jax.ShapeDtypeStruct((b, 1, LRU_WIDTH), F32)],
        compiler_params=_params(("arbitrary",)),
        name="lru_sample",
    )(zs3, zs3, state_conv, state_lru4, *lru_w)


def _split3(x):
    hi = x.astype(BF16)
    r = x - hi.astype(F32)
    mid = r.astype(BF16)
    lo = (r - mid.astype(F32)).astype(BF16)
    return hi, mid, lo


def _pad_rows(x, n):
    if x.shape[0] == n:
        return x
    return jnp.concatenate([x, jnp.zeros((n - x.shape[0], x.shape[1]), x.dtype)], axis=0)


def _lower_bound(logits, layer):
    e = jnp.exp(logits - jnp.max(logits, axis=0, keepdims=True))
    den = jnp.sum(e, axis=0, keepdims=True)
    num = jnp.zeros_like(den)
    for r in range(1, layer + 1):
        num = num + e[r:r + 1, :]
    return num / den


def _chunk_cumsum(x, chunk):
    n = x.shape[0]
    if chunk == SUBLANES:
        x3 = x.reshape(n // SUBLANES, SUBLANES, x.shape[1])
        sub = lax.broadcasted_iota(jnp.int32, (1, SUBLANES, 1), 1)
        for sh in (1, 2, 4):
            x3 = x3 + jnp.where(sub >= sh, pltpu.roll(x3, sh, 1), 0.0)
        return x3.reshape(n, x.shape[1])
    row = lax.broadcasted_iota(jnp.int32, (n, 1), 0)
    col = lax.broadcasted_iota(jnp.int32, (1, n), 1)
    tri = ((row >= col) & (row // chunk == col // chunk)).astype(BF16)
    out = None
    for part in _split3(x):
        term = jnp.dot(tri, part, preferred_element_type=F32)
        out = term if out is None else out + term
    return out


def _hgrn_tile(qc2, fc2, v, gc, st, lb2, cg, chunk):
    n = v.shape[0]
    nc = n // chunk
    nblk = chunk // SUBLANES
    lane = lax.broadcasted_iota(jnp.int32, (1, LANES), 1)
    head_a = lane < C_VDIM
    sub = lax.broadcasted_iota(jnp.int32, (1, SUBLANES, 1), 1)
    blocked = lambda x: x.reshape(n // SUBLANES, SUBLANES, LANES)
    chunked = lambda x: x.reshape(nc, chunk, LANES)
    bnt = (((2,), (2,)), ((0,), (0,)))
    bnn = (((2,), (1,)), ((0,), (0,)))
    nt = (((1,), (1,)), ((), ()))

    q2 = _silu(qc2) * (C_KDIM ** -0.5)
    g2 = lb2 + (1.0 - lb2) * jax.nn.sigmoid(fc2)
    k2 = 1.0 - g2
    b2 = _chunk_cumsum(jnp.log(g2) * LOG2_E, chunk)
    v3 = blocked(v)
    vc = chunked(v).astype(BF16)
    local, q_states, decays, k_ends = [], [], [], []
    for a in range(2):
        hs = slice(a * C_KDIM, (a + 1) * C_KDIM)
        q, k, b = q2[:, hs], k2[:, hs], b2[:, hs]
        q3, k3, b3 = blocked(q), blocked(k), blocked(b)
        o3 = jnp.sum(q3 * k3, axis=-1, keepdims=True) * v3
        for d in range(1, SUBLANES):
            w = q3 * pltpu.roll(k3, d, 1) * jnp.exp2(jnp.minimum(b3 - pltpu.roll(b3, d, 1), 0.0))
            a_d = jnp.where(sub >= d, jnp.sum(w, axis=-1, keepdims=True), 0.0)
            o3 = o3 + a_d * pltpu.roll(v3, d, 1)
        o = o3.reshape(n, LANES)
        qc_, kc_, bc_ = chunked(q), chunked(k), chunked(b)
        if nblk > 1:
            blocks = [jnp.zeros((nc, SUBLANES, chunk), F32)]
            for i in range(1, nblk):
                lo = i * SUBLANES
                ref = bc_[:, lo - 1:lo, :]
                kt = kc_[:, :lo, :] * jnp.exp2(jnp.minimum(ref - bc_[:, :lo, :], 0.0))
                kt = jnp.concatenate([kt, jnp.zeros((nc, chunk - lo, LANES), F32)], axis=1).astype(BF16)
                qt = (qc_[:, lo:lo + SUBLANES, :] * jnp.exp2(bc_[:, lo:lo + SUBLANES, :] - ref)).astype(BF16)
                blocks.append(lax.dot_general(qt, kt, bnt, preferred_element_type=F32))
            attn = jnp.concatenate(blocks, axis=1).astype(BF16)
            o = o + lax.dot_general(attn, vc, bnn, preferred_element_type=F32).reshape(n, LANES)
        local.append(o)
        last = bc_[:, chunk - 1:chunk, :]
        decays.append(jnp.exp2(last))
        k_ends.append(kc_ * jnp.exp2(last - bc_))
        q_states.append((qc_ * jnp.exp2(bc_)).astype(BF16))
    top = lax.broadcasted_iota(jnp.int32, (LANES, 1), 0) < C_VDIM
    pad = lambda x: _pad_rows(x, LANES).astype(BF16)
    inter = [[], []]
    for c in range(nc):
        st_b = st.astype(BF16)
        for a in range(2):
            inter[a].append(lax.dot_general(q_states[a][c], st_b, nt, preferred_element_type=F32))
        v_t = _pad_rows(v[c * chunk:(c + 1) * chunk], LANES).T.astype(BF16)
        upd_a = jnp.dot(v_t, pad(k_ends[0][c]), preferred_element_type=F32)
        upd_b = jnp.dot(v_t, pad(k_ends[1][c]), preferred_element_type=F32)
        st = jnp.where(top, st * decays[0][c] + upd_a, st * decays[1][c] + upd_b)
    cat = lambda xs: xs[0] if len(xs) == 1 else jnp.concatenate(xs, axis=0)
    o = jnp.where(head_a, local[0] + cat(inter[0]), local[1] + cat(inter[1]))
    sq = o * o
    ms_a = jnp.sum(jnp.where(head_a, sq, 0.0), axis=-1, keepdims=True) * (1.0 / C_VDIM)
    ms_b = jnp.sum(jnp.where(head_a, 0.0, sq), axis=-1, keepdims=True) * (1.0 / C_VDIM)
    inv = jnp.where(head_a, lax.rsqrt(ms_a + RMS_EPS), lax.rsqrt(ms_b + RMS_EPS))
    return o * inv * cg * _silu(gc), st


def _hgrn_prompt_kernel(qc_ref, fc_ref, ic_ref, gc_ref, lbl_ref, cg_ref, o_ref, st_ref, st_scr, *, layer, tile):
    t = pl.program_id(2)

    @pl.when(t == 0)
    def _():
        st_scr[...] = jnp.zeros_like(st_scr)

    lb2 = _lower_bound(lbl_ref[...], layer)
    cg = cg_ref[...]

    def body(c, carry):
        r0 = pl.multiple_of(c * HGRN_GROUP, HGRN_GROUP)
        rows = pl.ds(r0, HGRN_GROUP)
        out, st_new = _hgrn_tile(qc_ref[rows, :], fc_ref[rows, :], ic_ref[rows, :], gc_ref[rows, :],
                                 st_scr[...], lb2, cg, HGRN_CHUNK)
        o_ref[rows, :] = out.astype(o_ref.dtype)
        st_scr[...] = st_new
        return carry

    lax.fori_loop(0, tile // HGRN_GROUP, body, 0)

    @pl.when(t == pl.num_programs(2) - 1)
    def _():
        st_ref[...] = st_scr[...]


def _hgrn_prompt(z3, lb_logits, cnorm3, layer):
    b, s, _ = z3.shape
    tile = min(HGRN_TILE, s)
    assert s % tile == 0 and tile % HGRN_CHUNK == 0
    kw, vw = 2 * C_KDIM, 2 * C_VDIM
    return pl.pallas_call(
        functools.partial(_hgrn_prompt_kernel, layer=layer, tile=tile),
        grid=(b, HEAD_PAIRS, s // tile),
        in_specs=[pl.BlockSpec((None, tile, kw), lambda bi, p, t: (bi, t, COL_QC // kw + p)),
                  pl.BlockSpec((None, tile, kw), lambda bi, p, t: (bi, t, COL_FC // kw + p)),
                  pl.BlockSpec((None, tile, vw), lambda bi, p, t: (bi, t, COL_IC // vw + p)),
                  pl.BlockSpec((None, tile, vw), lambda bi, p, t: (bi, t, COL_GC // vw + p)),
                  pl.BlockSpec((DEPTH, kw), lambda bi, p, t: (0, p)),
                  pl.BlockSpec((None, 1, vw), lambda bi, p, t: (layer, 0, p))],
        out_specs=[pl.BlockSpec((None, tile, vw), lambda bi, p, t: (bi, t, p)),
                   pl.BlockSpec((None, None, LANES, LANES), lambda bi, p, t: (bi, p, 0, 0))],
        out_shape=[jax.ShapeDtypeStruct((b, s, C_VWIDTH), BF16),
                   jax.ShapeDtypeStruct((b, HEAD_PAIRS, LANES, LANES), F32)],
        scratch_shapes=[pltpu.VMEM((LANES, LANES), F32)],
        compiler_params=_params(("parallel", "parallel", "arbitrary")),
        name="hgrn_prompt",
    )(z3, z3, z3, z3, lb_logits, cnorm3)


def _hgrn_sample_kernel(qc_ref, fc_ref, ic_ref, gc_ref, st_in_ref, lbl_ref, cg_ref, o_ref, st_ref, *, layer):
    lb = _lower_bound(lbl_ref[...], layer)
    kw, vw = 2 * C_KDIM, 2 * C_VDIM
    for p in range(HEAD_PAIRS):
        ks, vs = slice(p * kw, (p + 1) * kw), slice(p * vw, (p + 1) * vw)
        out, st_new = _hgrn_tile(qc_ref[:, ks], fc_ref[:, ks], ic_ref[:, vs], gc_ref[:, vs], st_in_ref[p],
                                 lb[:, ks], cg_ref[:, vs], SUBLANES)
        o_ref[:, vs] = out
        st_ref[p] = st_new


def _hgrn_sample(zs3, state_t, lb_logits, cnorm3, layer):
    b, t_len, _ = zs3.shape
    assert t_len == SUBLANES
    return pl.pallas_call(
        functools.partial(_hgrn_sample_kernel, layer=layer),
        grid=(b,),
        in_specs=[pl.BlockSpec((None, t_len, C_KWIDTH), lambda bi: (bi, 0, COL_QC // C_KWIDTH)),
                  pl.BlockSpec((None, t_len, C_KWIDTH), lambda bi: (bi, 0, COL_FC // C_KWIDTH)),
                  pl.BlockSpec((None, t_len, C_VWIDTH), lambda bi: (bi, 0, COL_IC // C_VWIDTH)),
                  pl.BlockSpec((None, t_len, C_VWIDTH), lambda bi: (bi, 0, COL_GC // C_VWIDTH)),
                  pl.BlockSpec((None, None, HEAD_PAIRS, LANES, LANES), lambda bi: (layer, bi, 0, 0, 0)),
                  pl.BlockSpec((DEPTH, C_KWIDTH), lambda bi: (0, 0)),
                  pl.BlockSpec((None, 1, C_VWIDTH), lambda bi: (layer, 0, 0))],
        out_specs=[pl.BlockSpec((None, t_len, C_VWIDTH), lambda bi: (bi, 0, 0)),
                   pl.BlockSpec((None, HEAD_PAIRS, LANES, LANES), lambda bi: (bi, 0, 0, 0))],
        out_shape=[jax.ShapeDtypeStruct((b, t_len, C_VWIDTH), F32),
                   jax.ShapeDtypeStruct((b, HEAD_PAIRS, LANES, LANES), F32)],
        compiler_params=_params(("parallel",)),
        name="hgrn_sample",
    )(zs3, zs3, zs3, zs3, state_t, lb_logits, cnorm3)


def _block_diag(w):
    depth, nb, d, _ = w.shape
    eye = jnp.eye(nb, dtype=w.dtype)
    full = jnp.einsum("lhij,hg->lhigj", w, eye)
    return full.reshape(depth, nb * d, nb * d).astype(BF16)


def _state_to_pairs(st):
    lead = st.shape[:-3]
    t = jnp.swapaxes(st, -1, -2)
    return t.reshape(lead + (HEAD_PAIRS, 2 * C_VDIM, C_KDIM))


def _pairs_to_state(sp):
    lead = sp.shape[:-3]
    t = sp.reshape(lead + (C_HEADS, C_VDIM, C_KDIM))
    return jnp.swapaxes(t, -1, -2)


def kernel(x_prompt, x_sample, cache_k_win, cache_v_win, state_conv, state_lru, state_hgrn, norm_g, w_in, conv_w,
           conv_b, lru_w_a, lru_b_a, lru_w_x, lru_b_x, lru_lambda, hgrn_lb_logits, hgrn_norm_g, w_out, final_norm_g):
    bp, sp, _ = x_prompt.shape
    bs, ts, _ = x_sample.shape
    w_buf = cache_k_win.shape[2]
    assert w_buf == MAX_WINDOW and sp <= MAX_WINDOW

    w_in_bf = w_in.astype(BF16)
    w_out_bf = w_out.astype(BF16)
    norm_g3 = norm_g.reshape(DEPTH, 1, D_MODEL)
    final_g2 = final_norm_g.reshape(1, D_MODEL)
    row3 = lambda p: p.reshape(DEPTH, 1, -1)
    lru_w = (conv_w, row3(conv_b), _block_diag(lru_w_a), row3(lru_b_a), _block_diag(lru_w_x), row3(lru_b_x),
             row3(lru_lambda))
    cnorm3 = row3(hgrn_norm_g)
    cache_kt = jnp.transpose(cache_k_win, (0, 1, 3, 4, 2))
    cache_vt = jnp.transpose(cache_v_win, (0, 1, 3, 4, 2))
    state_lru4 = state_lru.reshape(DEPTH, bs, 1, LRU_WIDTH)
    state_t = _state_to_pairs(state_hgrn)
    bias_p = _prompt_bias()
    bias_s = _sample_bias(ts)

    xp = x_prompt.reshape(bp * sp, D_MODEL)
    xs = x_sample.reshape(bs * ts, D_MODEL)
    outs = {k: [] for k in ("ks", "vs", "cp", "cs", "lp", "ls", "hp", "hs")}
    kt_all = jnp.zeros((DEPTH, bp, A_HEADS, HEAD_DIM, sp), F32)
    vt_all = jnp.zeros((DEPTH, bp, A_HEADS, HEAD_DIM, sp), F32)
    for layer in range(DEPTH):
        last = layer == DEPTH - 1
        z = _inproj(xp, norm_g3, w_in_bf, layer)
        z3 = z.reshape(bp, sp, IN_WIDTH)
        mix_a, kt_all, vt_all = _attn_prompt(z3, bias_p, kt_all, vt_all, layer)
        mix_b, h_last = _lru_prompt(z3, lru_w, layer)
        mix_c, st_p = _hgrn_prompt(z3, hgrn_lb_logits, cnorm3, layer)
        xp = _outproj(xp, mix_a.reshape(bp * sp, -1), mix_b.reshape(bp * sp, -1), mix_c.reshape(bp * sp, -1),
                      w_out_bf, final_g2, layer, last)
        outs["cp"].append(z3[:, sp - (CONV_W - 1):, COL_XB:COL_XB + LRU_WIDTH])
        outs["lp"].append(h_last.reshape(bp, LRU_WIDTH))
        outs["hp"].append(_pairs_to_state(st_p))
        zs = _inproj(xs, norm_g3, w_in_bf, layer)
        zs3 = zs.reshape(bs, ts, IN_WIDTH)
        smix_a = _attn_sample(zs3, cache_kt, cache_vt, bias_s, layer)
        smix_b, sh_last = _lru_sample(zs3, state_conv, state_lru4, lru_w, layer)
        smix_c, st_s = _hgrn_sample(zs3, state_t, hgrn_lb_logits, cnorm3, layer)
        xs = _outproj(xs, smix_a.reshape(bs * ts, -1), smix_b.reshape(bs * ts, -1), smix_c.reshape(bs * ts, -1),
                      w_out_bf, final_g2, layer, last)
        outs["ks"].append(zs3[:, :, COL_KA:COL_KA + A_WIDTH].reshape(bs, ts, A_HEADS, HEAD_DIM))
        outs["vs"].append(zs3[:, :, COL_VA:COL_VA + A_WIDTH].reshape(bs, ts, A_HEADS, HEAD_DIM))
        outs["cs"].append(zs3[:, ts - (CONV_W - 1):, COL_XB:COL_XB + LRU_WIDTH])
        outs["ls"].append(sh_last.reshape(bs, LRU_WIDTH))
        outs["hs"].append(_pairs_to_state(st_s))
    st = lambda k: jnp.stack(outs[k])
    k_win_p = jnp.transpose(kt_all, (0, 1, 4, 2, 3))
    v_win_p = jnp.transpose(vt_all, (0, 1, 4, 2, 3))
    return (xp.reshape(bp, sp, D_MODEL), xs.reshape(bs, ts, D_MODEL), k_win_p, v_win_p, st("ks"), st("vs"),
            st("cp"), st("cs"), st("lp"), st("ls"), st("hp"), st("hs"))
```

```python
import functools

import jax
import jax.numpy as jnp
import numpy as np
from jax import lax
from jax.experimental import pallas as pl
from jax.experimental.pallas import tpu as pltpu

F32 = jnp.float32
BF16 = jnp.bfloat16

D_MODEL = 1024
DEPTH = 4
HEAD_DIM = 64
A_HEADS = 6
A_WIDTH = A_HEADS * HEAD_DIM
DILATED = ((128, 1), (512, 4), (2048, 16))
MAX_WINDOW = 2048
LRU_BLOCKS = 6
LRU_BLOCK_DIM = 64
LRU_WIDTH = LRU_BLOCKS * LRU_BLOCK_DIM
CONV_W = 4
LRU_C = 8.0
C_HEADS = 6
C_KDIM = 128
C_VDIM = 64
C_KWIDTH = C_HEADS * C_KDIM
C_VWIDTH = C_HEADS * C_VDIM
MIX_WIDTH = A_WIDTH + LRU_WIDTH + C_VWIDTH
IN_WIDTH = 4 * A_WIDTH + 2 * LRU_WIDTH + 2 * C_KWIDTH + 2 * C_VWIDTH
RMS_EPS = 1e-6
LOG2_E = 1.4426950408889634

LANES = 128
SUBLANES = 8
HEAD_PAIRS = A_HEADS // 2
VMEM_LIMIT = 56 * 1024 * 1024

COL_QA, COL_KA, COL_VA, COL_GA = 0, A_WIDTH, 2 * A_WIDTH, 3 * A_WIDTH
COL_XB = 4 * A_WIDTH
COL_GB = COL_XB + LRU_WIDTH
COL_QC = COL_GB + LRU_WIDTH
COL_FC = COL_QC + C_KWIDTH
COL_IC = COL_FC + C_KWIDTH
COL_GC = COL_IC + C_VWIDTH

ATT_TQ = 256
ATT_TK = 512
HGRN_CHUNK = 64
HGRN_GROUP = 256
HGRN_TILE = 512
HGRN_SAMPLE_ROWS = 8
LRU_TILE = 512
PROJ_TM = 512


def _silu(x):
    return x * jax.nn.sigmoid(x)


def _params(sem, **kw):
    return pltpu.CompilerParams(dimension_semantics=sem, vmem_limit_bytes=VMEM_LIMIT, **kw)


def _inproj_kernel(x_ref, g_ref, w_ref, z_ref, *, n_chunk):
    x = x_ref[...]
    h = (x * lax.rsqrt(jnp.mean(x * x, axis=-1, keepdims=True) + RMS_EPS) * g_ref[...]).astype(BF16)
    for n in range(IN_WIDTH // n_chunk):
        sl = slice(n * n_chunk, (n + 1) * n_chunk)
        z_ref[:, sl] = jnp.dot(h, w_ref[:, sl], preferred_element_type=F32)


def _inproj(x2d, norm_g3, w_in_bf, layer):
    m = x2d.shape[0]
    tm = min(PROJ_TM, m)
    assert m % tm == 0
    return pl.pallas_call(
        functools.partial(_inproj_kernel, n_chunk=512),
        grid=(m // tm,),
        in_specs=[
            pl.BlockSpec((tm, D_MODEL), lambda i: (i, 0)),
            pl.BlockSpec((None, 1, D_MODEL), lambda i: (layer, 0, 0)),
            pl.BlockSpec((None, D_MODEL, IN_WIDTH), lambda i: (layer, 0, 0)),
        ],
        out_specs=pl.BlockSpec((tm, IN_WIDTH), lambda i: (i, 0)),
        out_shape=jax.ShapeDtypeStruct((m, IN_WIDTH), F32),
        compiler_params=_params(("parallel",)),
        name="inproj",
    )(x2d, norm_g3, w_in_bf)


def _outproj_kernel(x_ref, ma_ref, mb_ref, mc_ref, w_ref, g_ref, y_ref, *, final_norm):
    y = x_ref[...]
    y = y + jnp.dot(ma_ref[...].astype(BF16), w_ref[0:A_WIDTH, :], preferred_element_type=F32)
    y = y + jnp.dot(mb_ref[...].astype(BF16), w_ref[A_WIDTH:A_WIDTH + LRU_WIDTH, :], preferred_element_type=F32)
    y = y + jnp.dot(mc_ref[...].astype(BF16), w_ref[A_WIDTH + LRU_WIDTH:, :], preferred_element_type=F32)
    if final_norm:
        y = y * lax.rsqrt(jnp.mean(y * y, axis=-1, keepdims=True) + RMS_EPS) * g_ref[...]
    y_ref[...] = y


def _outproj(x2d, mix_a, mix_b, mix_c, w_out_bf, final_g2, layer, final_norm):
    m = x2d.shape[0]
    tm = min(PROJ_TM, m)
    assert m % tm == 0
    mspec = lambda w: pl.BlockSpec((tm, w), lambda i: (i, 0))
    return pl.pallas_call(
        functools.partial(_outproj_kernel, final_norm=final_norm),
        grid=(m // tm,),
        in_specs=[
            mspec(D_MODEL), mspec(A_WIDTH), mspec(LRU_WIDTH), mspec(C_VWIDTH),
            pl.BlockSpec((None, MIX_WIDTH, D_MODEL), lambda i: (layer, 0, 0)),
            pl.BlockSpec((1, D_MODEL), lambda i: (0, 0)),
        ],
        out_specs=mspec(D_MODEL),
        out_shape=jax.ShapeDtypeStruct((m, D_MODEL), F32),
        compiler_params=_params(("parallel",)),
        name="outproj",
    )(x2d, mix_a, mix_b, mix_c, w_out_bf, final_g2)


def _branch_count(delta):
    delta = np.asarray(delta)
    cnt = np.zeros(delta.shape, np.int32)
    for window, dil in DILATED:
        cnt += ((delta >= 0) & (delta <= window) & (delta % dil == 0)).astype(np.int32)
    return cnt


def _log_count(cnt):
    with np.errstate(divide="ignore"):
        return np.where(cnt > 0, np.log(np.maximum(cnt, 1).astype(np.float64)), -np.inf).astype(np.float32)


ATT_FAR_OFF = (DILATED[1][0] + ATT_TK - 1) // ATT_TQ + 1


def _prompt_bias():
    qi = np.arange(ATT_TQ)[None, :]
    ki = np.arange(ATT_TK)[:, None]
    tabs = []
    for off in range(ATT_FAR_OFF + 1):
        b = _log_count(_branch_count(off * ATT_TQ + qi - ki)) * np.float32(LOG2_E)
        tabs.append(np.concatenate([b, b], axis=1))
    return jnp.asarray(np.stack(tabs))


def _attn_pair_kernel(qa_ref, qb_ref, k_ref, v_ref, ga_ref, gb_ref, bias_ref, kt_in_ref, vt_in_ref,
                      oa_ref, ob_ref, kt_ref, vt_ref, *, n_q):
    t = pl.program_id(2)
    lane = lax.broadcasted_iota(jnp.int32, (1, LANES), 1)
    head_a = lane < HEAD_DIM
    ratio = ATT_TK // ATT_TQ
    nt = (((1,), (1,)), ((), ()))
    tn = (((0,), (0,)), ((), ()))

    def stacked(q_ref):
        q = q_ref[...] * (HEAD_DIM ** -0.5 * LOG2_E)
        return jnp.concatenate([jnp.where(head_a, q, 0.0), jnp.where(head_a, 0.0, q)], axis=0).astype(BF16)

    def step(q2, i, j, carry):
        m, l, acc = carry
        k0, k1 = j * ATT_TK, min((j + 1) * ATT_TK, (i + 1) * ATT_TQ)
        kb = k_ref[k0:k1, :].astype(BF16)
        vb = v_ref[k0:k1, :].astype(BF16)
        bias = bias_ref[min(i - j * ratio, ATT_FAR_OFF)][:k1 - k0]
        s = lax.dot_general(kb, q2, nt, preferred_element_type=F32) + bias
        m_new = jnp.maximum(m, jnp.max(s, axis=0, keepdims=True))
        alpha = jnp.exp2(m - m_new)
        p = jnp.exp2(s - m_new)
        l = alpha * l + jnp.sum(p, axis=0, keepdims=True)
        pv = lax.dot_general(vb, p.astype(BF16), tn, preferred_element_type=F32)
        return m_new, l, alpha * acc + pv

    def finish(carry, g_ref):
        _, l, acc = carry
        o2 = (acc / l).T
        o = jnp.where(head_a, o2[:ATT_TQ], o2[ATT_TQ:])
        return (o * _silu(g_ref[...])).astype(BF16)

    def run(tt):
        i_a, i_b = tt, n_q - 1 - tt
        n_a, n_b = (i_a * ATT_TQ) // ATT_TK + 1, (i_b * ATT_TQ) // ATT_TK + 1
        q2a, q2b = stacked(qa_ref), stacked(qb_ref)
        init = (jnp.full((1, 2 * ATT_TQ), -jnp.inf, F32), jnp.zeros((1, 2 * ATT_TQ), F32),
                jnp.zeros((LANES, 2 * ATT_TQ), F32))
        ca, cb = init, init
        for j in range(max(n_a, n_b)):
            if j < n_b:
                cb = step(q2b, i_b, j, cb)
            if j < n_a:
                ca = step(q2a, i_a, j, ca)
        return finish(ca, ga_ref), finish(cb, gb_ref)

    oa, ob = lax.switch(t, [functools.partial(run, tt) for tt in range(n_q // 2)])
    oa_ref[...] = oa
    ob_ref[...] = ob

    @pl.when(t == 0)
    def _():
        s_len = k_ref.shape[0]
        for src, dst in ((k_ref, kt_ref), (v_ref, vt_ref)):
            for c in range(s_len // LANES):
                blk = src[c * LANES:(c + 1) * LANES, :].T
                dst[:, :, c * LANES:(c + 1) * LANES] = blk.reshape(2, HEAD_DIM, LANES)


def _attn_prompt(z3, bias, kt_all, vt_all, layer):
    b, s, _ = z3.shape
    n_q = s // ATT_TQ
    assert ATT_TK % ATT_TQ == 0 and s % ATT_TK == 0 and s <= MAX_WINDOW and ATT_TQ % DILATED[2][1] == 0
    assert n_q % 2 == 0
    half = n_q // 2
    col = lambda base: (lambda bi, p, t: (bi, 0, base // LANES + p))
    fwd = lambda base: pl.BlockSpec((None, ATT_TQ, LANES), lambda bi, p, t: (bi, t, base // LANES + p))
    rev = lambda base: pl.BlockSpec((None, ATT_TQ, LANES), lambda bi, p, t: (bi, n_q - 1 - t, base // LANES + p))
    win = pl.BlockSpec((None, None, 2, HEAD_DIM, s), lambda bi, p, t: (layer, bi, p, 0, 0))
    half_out = pl.BlockSpec((None, ATT_TQ, LANES), lambda bi, p, t: (bi, t, p))
    o_lo, o_hi, kt_all, vt_all = pl.pallas_call(
        functools.partial(_attn_pair_kernel, n_q=n_q),
        grid=(b, HEAD_PAIRS, half),
        in_specs=[
            fwd(COL_QA), rev(COL_QA),
            pl.BlockSpec((None, s, LANES), col(COL_KA)),
            pl.BlockSpec((None, s, LANES), col(COL_VA)),
            fwd(COL_GA), rev(COL_GA),
            pl.BlockSpec((ATT_FAR_OFF + 1, ATT_TK, 2 * ATT_TQ), lambda bi, p, t: (0, 0, 0)),
            pl.BlockSpec(memory_space=pl.ANY),
            pl.BlockSpec(memory_space=pl.ANY),
        ],
        out_specs=[half_out, half_out, win, win],
        out_shape=[jax.ShapeDtypeStruct((b, half * ATT_TQ, A_WIDTH), BF16),
                   jax.ShapeDtypeStruct((b, half * ATT_TQ, A_WIDTH), BF16),
                   jax.ShapeDtypeStruct(kt_all.shape, F32), jax.ShapeDtypeStruct(vt_all.shape, F32)],
        input_output_aliases={7: 2, 8: 3},
        compiler_params=_params(("parallel", "parallel", "arbitrary")),
        name="attn_prompt",
    )(z3, z3, z3, z3, z3, z3, bias, kt_all, vt_all)
    o_hi = o_hi.reshape(b, half, ATT_TQ, A_WIDTH)[:, ::-1].reshape(b, half * ATT_TQ, A_WIDTH)
    return jnp.concatenate([o_lo, o_hi], axis=1), kt_all, vt_all


def _sample_bias(t_len):
    qpos = MAX_WINDOW + np.arange(t_len)[:, None]
    bias_c = _log_count(_branch_count(qpos - np.arange(MAX_WINDOW)[None, :]))
    bias_n = _log_count(_branch_count(qpos - (MAX_WINDOW + np.arange(t_len)[None, :])))
    return jnp.asarray(bias_c), jnp.asarray(bias_n)


def _split_heads(x):
    return jnp.stack([x[:, h * HEAD_DIM:(h + 1) * HEAD_DIM] for h in range(A_HEADS)], axis=0)


def _attn_sample_kernel(q_ref, k_ref, v_ref, g_ref, kt_ref, vt_ref, bias_c_ref, bias_n_ref, o_ref):
    q3 = _split_heads(q_ref[...] * (HEAD_DIM ** -0.5))
    kn3 = _split_heads(k_ref[...])
    vn3 = _split_heads(v_ref[...])
    bdot = lambda a, b, ca, cb: lax.dot_general(a, b, (((ca,), (cb,)), ((0,), (0,))), preferred_element_type=F32)
    s_c = bdot(q3, kt_ref[...], 2, 1) + bias_c_ref[...][None]
    s_n = bdot(q3, kn3, 2, 2) + bias_n_ref[...][None]
    m = jnp.maximum(jnp.max(s_c, axis=-1, keepdims=True), jnp.max(s_n, axis=-1, keepdims=True))
    p_c = jnp.exp(s_c - m)
    p_n = jnp.exp(s_n - m)
    l = jnp.sum(p_c, axis=-1, keepdims=True) + jnp.sum(p_n, axis=-1, keepdims=True)
    acc = bdot(p_c, vt_ref[...], 2, 2) + bdot(p_n, vn3, 2, 1)
    o3 = acc / l
    o = jnp.concatenate([o3[h] for h in range(A_HEADS)], axis=-1)
    o_ref[...] = o * _silu(g_ref[...])


def _attn_sample(zs3, cache_kt, cache_vt, bias, layer):
    b, t_len, _ = zs3.shape
    w_buf = cache_kt.shape[-1]
    zcol = lambda base: pl.BlockSpec((None, t_len, A_WIDTH), lambda bi: (bi, 0, base // A_WIDTH))
    cache = pl.BlockSpec((None, None, A_HEADS, HEAD_DIM, w_buf), lambda bi: (layer, bi, 0, 0, 0))
    bias_c, bias_n = bias
    return pl.pallas_call(
        _attn_sample_kernel,
        grid=(b,),
        in_specs=[zcol(COL_QA), zcol(COL_KA), zcol(COL_VA), zcol(COL_GA), cache, cache,
                  pl.BlockSpec(bias_c.shape, lambda bi: (0, 0)), pl.BlockSpec(bias_n.shape, lambda bi: (0, 0))],
        out_specs=pl.BlockSpec((None, t_len, A_WIDTH), lambda bi: (bi, 0, 0)),
        out_shape=jax.ShapeDtypeStruct((b, t_len, A_WIDTH), F32),
        compiler_params=_params(("parallel",)),
        name="attn_sample",
    )(zs3, zs3, zs3, zs3, cache_kt, cache_vt, bias_c, bias_n)


def _neg_expm1(x):
    series = -x * (1.0 + x * (0.5 + x * (1.0 / 6.0 + x * (1.0 / 24.0 + x * (1.0 / 120.0)))))
    return jnp.where(x > -0.1, series, 1.0 - jnp.exp(x))


def _softplus(y):
    return jnp.maximum(y, 0.0) + jnp.log1p(jnp.exp(-jnp.abs(y)))


def _lru_gates(xc, wa_ref, ba_ref, wx_ref, bx_ref, lam_ref):
    xcb = xc.astype(BF16)
    r = jax.nn.sigmoid(jnp.dot(xcb, wa_ref[...], preferred_element_type=F32) + ba_ref[...])
    i = jax.nn.sigmoid(jnp.dot(xcb, wx_ref[...], preferred_element_type=F32) + bx_ref[...])
    log_a = -LRU_C * r * _softplus(-lam_ref[...])
    a = jnp.exp(log_a)
    u = jnp.sqrt(_neg_expm1(2.0 * log_a)) * (i * xc)
    return a, u


def _scan8(a, u, axis, row):
    for sh in (1, 2, 4):
        keep = row >= sh
        a_prev = jnp.where(keep, pltpu.roll(a, sh, axis), 1.0)
        u_prev = jnp.where(keep, pltpu.roll(u, sh, axis), 0.0)
        u = a * u_prev + u
        a = a * a_prev
    return a, u


def _lru_prompt_kernel(xb_ref, gb_ref, cw_ref, cb_ref, wa_ref, ba_ref, wx_ref, bx_ref, lam_ref,
                       o_ref, hlast_ref, xp_scr, a_scr, u_scr, hc_scr, *, tile):
    t = pl.program_id(1)

    @pl.when(t == 0)
    def _():
        xp_scr[0:SUBLANES, :] = jnp.zeros((SUBLANES, LRU_WIDTH), F32)
        hc_scr[...] = jnp.zeros_like(hc_scr)

    xp_scr[SUBLANES:SUBLANES + tile, :] = xb_ref[...]
    cw = cw_ref[...]
    xc = cb_ref[...]
    for j in range(CONV_W):
        off = SUBLANES - (CONV_W - 1) + j
        xc = xc + xp_scr[off:off + tile, :] * cw[j:j + 1, :]
    xp_scr[0:SUBLANES, :] = xp_scr[tile:tile + SUBLANES, :]

    a, u = _lru_gates(xc, wa_ref, ba_ref, wx_ref, bx_ref, lam_ref)
    row = lax.broadcasted_iota(jnp.int32, (1, SUBLANES, 1), 1)
    grouped = (tile // SUBLANES, SUBLANES, LRU_WIDTH)
    a, u = _scan8(a.reshape(grouped), u.reshape(grouped), 1, row)
    a_scr[...] = a.reshape(tile, LRU_WIDTH)
    u_scr[...] = u.reshape(tile, LRU_WIDTH)

    def body(gi, hc):
        r0 = pl.multiple_of(gi * SUBLANES, SUBLANES)
        h = a_scr[pl.ds(r0, SUBLANES), :] * hc + u_scr[pl.ds(r0, SUBLANES), :]
        u_scr[pl.ds(r0, SUBLANES), :] = h
        return jnp.broadcast_to(h[SUBLANES - 1:SUBLANES, :], (SUBLANES, LRU_WIDTH))

    hc = lax.fori_loop(0, tile // SUBLANES, body, hc_scr[...])
    hc_scr[...] = hc
    o_ref[...] = (u_scr[...] * _silu(gb_ref[...])).astype(o_ref.dtype)

    @pl.when(t == pl.num_programs(1) - 1)
    def _():
        hlast_ref[...] = hc[0:1, :]


def _lru_weight_specs(layer, nargs):
    def spec(shape):
        idx = (layer,) + (0,) * len(shape)
        if nargs == 1:
            return pl.BlockSpec((None,) + shape, lambda a: idx)
        return pl.BlockSpec((None,) + shape, lambda a, b: idx)
    return [spec((CONV_W, LRU_WIDTH)), spec((1, LRU_WIDTH)),
            spec((LRU_WIDTH, LRU_WIDTH)), spec((1, LRU_WIDTH)),
            spec((LRU_WIDTH, LRU_WIDTH)), spec((1, LRU_WIDTH)), spec((1, LRU_WIDTH))]


def _lru_prompt(z3, lru_w, layer):
    b, s, _ = z3.shape
    tile = min(LRU_TILE, s)
    assert s % tile == 0
    return pl.pallas_call(
        functools.partial(_lru_prompt_kernel, tile=tile),
        grid=(b, s // tile),
        in_specs=[pl.BlockSpec((None, tile, LRU_WIDTH), lambda bi, t: (bi, t, COL_XB // LRU_WIDTH)),
                  pl.BlockSpec((None, tile, LRU_WIDTH), lambda bi, t: (bi, t, COL_GB // LRU_WIDTH))]
        + _lru_weight_specs(layer, 2),
        out_specs=[pl.BlockSpec((None, tile, LRU_WIDTH), lambda bi, t: (bi, t, 0)),
                   pl.BlockSpec((None, 1, LRU_WIDTH), lambda bi, t: (bi, 0, 0))],
        out_shape=[jax.ShapeDtypeStruct((b, s, LRU_WIDTH), BF16),
                   jax.ShapeDtypeStruct((b, 1, LRU_WIDTH), F32)],
        scratch_shapes=[pltpu.VMEM((tile + 2 * SUBLANES, LRU_WIDTH), F32),
                        pltpu.VMEM((tile, LRU_WIDTH), F32),
                        pltpu.VMEM((tile, LRU_WIDTH), F32),
                        pltpu.VMEM((SUBLANES, LRU_WIDTH), F32)],
        compiler_params=_params(("parallel", "arbitrary")),
        name="lru_prompt",
    )(z3, z3, *lru_w)


def _lru_sample_kernel(xb_ref, gb_ref, cs_ref, h0_ref, cw_ref, cb_ref, wa_ref, ba_ref, wx_ref, bx_ref, lam_ref,
                       o_ref, hlast_ref):
    x = xb_ref[...]
    nb, t_len, _ = x.shape
    cs = cs_ref[...]
    row = lax.broadcasted_iota(jnp.int32, (1, t_len, 1), 1)
    cw = cw_ref[...]
    xc = cb_ref[...] + x * cw[CONV_W - 1:CONV_W, :]
    for back in range(1, CONV_W):
        xs = pltpu.roll(x, back, 1)
        for r in range(back):
            src = CONV_W - 1 - back + r
            xs = jnp.where(row == r, cs[:, src:src + 1, :], xs)
        xc = xc + xs * cw[CONV_W - 1 - back:CONV_W - back, :]
    a, u = _lru_gates(xc.reshape(nb * t_len, LRU_WIDTH), wa_ref, ba_ref, wx_ref, bx_ref, lam_ref)
    a = a.reshape(nb, t_len, LRU_WIDTH)
    u = u.reshape(nb, t_len, LRU_WIDTH)
    a, u = _scan8(a, u, 1, row)
    h = a * h0_ref[...] + u
    o_ref[...] = h * _silu(gb_ref[...])
    hlast_ref[...] = h[:, t_len - 1:t_len, :]


def _lru_sample(zs3, state_conv, state_lru4, lru_w, layer):
    b, t_len, _ = zs3.shape
    assert t_len == SUBLANES
    return pl.pallas_call(
        _lru_sample_kernel,
        grid=(1,),
        in_specs=[pl.BlockSpec((b, t_len, LRU_WIDTH), lambda i: (0, 0, COL_XB // LRU_WIDTH)),
                  pl.BlockSpec((b, t_len, LRU_WIDTH), lambda i: (0, 0, COL_GB // LRU_WIDTH)),
                  pl.BlockSpec((None, b, CONV_W - 1, LRU_WIDTH), lambda i: (layer, 0, 0, 0)),
                  pl.BlockSpec((None, b, 1, LRU_WIDTH), lambda i: (layer, 0, 0, 0))]
        + _lru_weight_specs(layer, 1),
        out_specs=[pl.BlockSpec((b, t_len, LRU_WIDTH), lambda i: (0, 0, 0)),
                   pl.BlockSpec((b, 1, LRU_WIDTH), lambda i: (0, 0, 0))],
        out_shape=[jax.ShapeDtypeStruct((b, t_len, LRU_WIDTH), F32),
                   jax.ShapeDtypeStruct((b, 1, LRU_WIDTH), F32)],
        compiler_params=_params(("arbitrary",)),
        name="lru_sample",
    )(zs3, zs3, state_conv, state_lru4, *lru_w)


def _split3(x):
    hi = x.astype(BF16)
    r = x - hi.astype(F32)
    mid = r.astype(BF16)
    lo = (r - mid.astype(F32)).astype(BF16)
    return hi, mid, lo


def _pad_rows(x, n):
    if x.shape[0] == n:
        return x
    return jnp.concatenate([x, jnp.zeros((n - x.shape[0], x.shape[1]), x.dtype)], axis=0)


def _lower_bound(logits, layer):
    e = jnp.exp(logits - jnp.max(logits, axis=0, keepdims=True))
    den = jnp.sum(e, axis=0, keepdims=True)
    num = jnp.zeros_like(den)
    for r in range(1, layer + 1):
        num = num + e[r:r + 1, :]
    return num / den


def _chunk_cumsum(x, chunk):
    n = x.shape[0]
    if chunk == SUBLANES:
        x3 = x.reshape(n // SUBLANES, SUBLANES, x.shape[1])
        sub = lax.broadcasted_iota(jnp.int32, (1, SUBLANES, 1), 1)
        for sh in (1, 2, 4):
            x3 = x3 + jnp.where(sub >= sh, pltpu.roll(x3, sh, 1), 0.0)
        return x3.reshape(n, x.shape[1])
    row = lax.broadcasted_iota(jnp.int32, (n, 1), 0)
    col = lax.broadcasted_iota(jnp.int32, (1, n), 1)
    tri = ((row >= col) & (row // chunk == col // chunk)).astype(BF16)
    out = None
    for part in _split3(x):
        term = jnp.dot(tri, part, preferred_element_type=F32)
        out = term if out is None else out + term
    return out


def _hgrn_tile(qc2, fc2, v, gc, st, lb2, cg, chunk, carry=True):
    n = v.shape[0]
    nc = n // chunk
    nblk = chunk // SUBLANES
    lane = lax.broadcasted_iota(jnp.int32, (1, LANES), 1)
    head_a = lane < C_VDIM
    sub = lax.broadcasted_iota(jnp.int32, (1, SUBLANES, 1), 1)
    blocked = lambda x: x.reshape(n // SUBLANES, SUBLANES, LANES)
    chunked = lambda x: x.reshape(nc, chunk, LANES)
    bnt = (((2,), (2,)), ((0,), (0,)))
    bnn = (((2,), (1,)), ((0,), (0,)))
    nt = (((1,), (1,)), ((), ()))

    q2 = _silu(qc2) * (C_KDIM ** -0.5)
    g2 = lb2 + (1.0 - lb2) * jax.nn.sigmoid(fc2)
    k2 = 1.0 - g2
    b2 = _chunk_cumsum(jnp.log(g2) * LOG2_E, chunk)
    v3 = blocked(v)
    vc = chunked(v).astype(BF16)
    local, q_states, decays, k_ends = [], [], [], []
    for a in range(2):
        hs = slice(a * C_KDIM, (a + 1) * C_KDIM)
        q, k, b = q2[:, hs], k2[:, hs], b2[:, hs]
        q3, k3, b3 = blocked(q), blocked(k), blocked(b)
        o3 = jnp.sum(q3 * k3, axis=-1, keepdims=True) * v3
        for d in range(1, SUBLANES):
            w = q3 * pltpu.roll(k3, d, 1) * jnp.exp2(jnp.minimum(b3 - pltpu.roll(b3, d, 1), 0.0))
            a_d = jnp.where(sub >= d, jnp.sum(w, axis=-1, keepdims=True), 0.0)
            o3 = o3 + a_d * pltpu.roll(v3, d, 1)
        o = o3.reshape(n, LANES)
        qc_, kc_, bc_ = chunked(q), chunked(k), chunked(b)
        if nblk > 1:
            blocks = [jnp.zeros((nc, SUBLANES, chunk), F32)]
            for i in range(1, nblk):
                lo = i * SUBLANES
                ref = bc_[:, lo - 1:lo, :]
                kt = kc_[:, :lo, :] * jnp.exp2(jnp.minimum(ref - bc_[:, :lo, :], 0.0))
                kt = jnp.concatenate([kt, jnp.zeros((nc, chunk - lo, LANES), F32)], axis=1).astype(BF16)
                qt = (qc_[:, lo:lo + SUBLANES, :] * jnp.exp2(bc_[:, lo:lo + SUBLANES, :] - ref)).astype(BF16)
                blocks.append(lax.dot_general(qt, kt, bnt, preferred_element_type=F32))
            attn = jnp.concatenate(blocks, axis=1).astype(BF16)
            o = o + lax.dot_general(attn, vc, bnn, preferred_element_type=F32).reshape(n, LANES)
        local.append(o)
        last = bc_[:, chunk - 1:chunk, :]
        decays.append(jnp.exp2(last))
        k_ends.append(kc_ * jnp.exp2(last - bc_))
        q_states.append((qc_ * jnp.exp2(bc_)).astype(BF16))
    top = lax.broadcasted_iota(jnp.int32, (LANES, 1), 0) < C_VDIM
    pad = lambda x: _pad_rows(x, LANES).astype(BF16)
    inter = [[], []]
    new_states = []
    for c in range(nc):
        st_c = st if carry else st[c]
        st_b = st_c.astype(BF16)
        for a in range(2):
            inter[a].append(lax.dot_general(q_states[a][c], st_b, nt, preferred_element_type=F32))
        v_t = _pad_rows(v[c * chunk:(c + 1) * chunk], LANES).T.astype(BF16)
        upd_a = jnp.dot(v_t, pad(k_ends[0][c]), preferred_element_type=F32)
        upd_b = jnp.dot(v_t, pad(k_ends[1][c]), preferred_element_type=F32)
        st_c = jnp.where(top, st_c * decays[0][c] + upd_a, st_c * decays[1][c] + upd_b)
        if carry:
            st = st_c
        else:
            new_states.append(st_c)
    if not carry:
        st = jnp.stack(new_states)
    cat = lambda xs: xs[0] if len(xs) == 1 else jnp.concatenate(xs, axis=0)
    o = jnp.where(head_a, local[0] + cat(inter[0]), local[1] + cat(inter[1]))
    sq = o * o
    ms_a = jnp.sum(jnp.where(head_a, sq, 0.0), axis=-1, keepdims=True) * (1.0 / C_VDIM)
    ms_b = jnp.sum(jnp.where(head_a, 0.0, sq), axis=-1, keepdims=True) * (1.0 / C_VDIM)
    inv = jnp.where(head_a, lax.rsqrt(ms_a + RMS_EPS), lax.rsqrt(ms_b + RMS_EPS))
    return o * inv * cg * _silu(gc), st


def _hgrn_prompt_kernel(qc_ref, fc_ref, ic_ref, gc_ref, lbl_ref, cg_ref, o_ref, st_ref, st_scr, *, layer, tile):
    t = pl.program_id(2)

    @pl.when(t == 0)
    def _():
        st_scr[...] = jnp.zeros_like(st_scr)

    lb2 = _lower_bound(lbl_ref[...], layer)
    cg = cg_ref[...]

    def body(c, carry):
        r0 = pl.multiple_of(c * HGRN_GROUP, HGRN_GROUP)
        rows = pl.ds(r0, HGRN_GROUP)
        out, st_new = _hgrn_tile(qc_ref[rows, :], fc_ref[rows, :], ic_ref[rows, :], gc_ref[rows, :],
                                 st_scr[...], lb2, cg, HGRN_CHUNK)
        o_ref[rows, :] = out.astype(o_ref.dtype)
        st_scr[...] = st_new
        return carry

    lax.fori_loop(0, tile // HGRN_GROUP, body, 0)

    @pl.when(t == pl.num_programs(2) - 1)
    def _():
        st_ref[...] = st_scr[...]


def _hgrn_prompt(z3, lb_logits, cnorm3, layer):
    b, s, _ = z3.shape
    tile = min(HGRN_TILE, s)
    assert s % tile == 0 and tile % HGRN_CHUNK == 0
    kw, vw = 2 * C_KDIM, 2 * C_VDIM
    return pl.pallas_call(
        functools.partial(_hgrn_prompt_kernel, layer=layer, tile=tile),
        grid=(b, HEAD_PAIRS, s // tile),
        in_specs=[pl.BlockSpec((None, tile, kw), lambda bi, p, t: (bi, t, COL_QC // kw + p)),
                  pl.BlockSpec((None, tile, kw), lambda bi, p, t: (bi, t, COL_FC // kw + p)),
                  pl.BlockSpec((None, tile, vw), lambda bi, p, t: (bi, t, COL_IC // vw + p)),
                  pl.BlockSpec((None, tile, vw), lambda bi, p, t: (bi, t, COL_GC // vw + p)),
                  pl.BlockSpec((DEPTH, kw), lambda bi, p, t: (0, p)),
                  pl.BlockSpec((None, 1, vw), lambda bi, p, t: (layer, 0, p))],
        out_specs=[pl.BlockSpec((None, tile, vw), lambda bi, p, t: (bi, t, p)),
                   pl.BlockSpec((None, None, LANES, LANES), lambda bi, p, t: (bi, p, 0, 0))],
        out_shape=[jax.ShapeDtypeStruct((b, s, C_VWIDTH), BF16),
                   jax.ShapeDtypeStruct((b, HEAD_PAIRS, LANES, LANES), F32)],
        scratch_shapes=[pltpu.VMEM((LANES, LANES), F32)],
        compiler_params=_params(("parallel", "parallel", "arbitrary")),
        name="hgrn_prompt",
    )(z3, z3, z3, z3, lb_logits, cnorm3)


def _hgrn_sample_kernel(qc_ref, fc_ref, ic_ref, gc_ref, st_in_ref, lbl_ref, cg_ref, o_ref, st_ref, *, layer):
    lb = _lower_bound(lbl_ref[...], layer)
    rows, t_len, _ = qc_ref.shape
    kw, vw = 2 * C_KDIM, 2 * C_VDIM
    flat = lambda ref, sl: ref[:, :, sl].reshape(rows * t_len, sl.stop - sl.start)
    for p in range(HEAD_PAIRS):
        ks, vs = slice(p * kw, (p + 1) * kw), slice(p * vw, (p + 1) * vw)
        out, st_new = _hgrn_tile(flat(qc_ref, ks), flat(fc_ref, ks), flat(ic_ref, vs), flat(gc_ref, vs),
                                 st_in_ref[:, p], lb[:, ks], cg_ref[:, vs], t_len, carry=False)
        o_ref[:, :, vs] = out.reshape(rows, t_len, vw)
        st_ref[:, p] = st_new


def _hgrn_sample(zs3, state_t, lb_logits, cnorm3, layer):
    b, t_len, _ = zs3.shape
    rows = min(HGRN_SAMPLE_ROWS, b)
    assert t_len == SUBLANES and b % rows == 0
    return pl.pallas_call(
        functools.partial(_hgrn_sample_kernel, layer=layer),
        grid=(b // rows,),
        in_specs=[pl.BlockSpec((rows, t_len, C_KWIDTH), lambda bi: (bi, 0, COL_QC // C_KWIDTH)),
                  pl.BlockSpec((rows, t_len, C_KWIDTH), lambda bi: (bi, 0, COL_FC // C_KWIDTH)),
                  pl.BlockSpec((rows, t_len, C_VWIDTH), lambda bi: (bi, 0, COL_IC // C_VWIDTH)),
                  pl.BlockSpec((rows, t_len, C_VWIDTH), lambda bi: (bi, 0, COL_GC // C_VWIDTH)),
                  pl.BlockSpec((None, rows, HEAD_PAIRS, LANES, LANES), lambda bi: (layer, bi, 0, 0, 0)),
                  pl.BlockSpec((DEPTH, C_KWIDTH), lambda bi: (0, 0)),
                  pl.BlockSpec((None, 1, C_VWIDTH), lambda bi: (layer, 0, 0))],
        out_specs=[pl.BlockSpec((rows, t_len, C_VWIDTH), lambda bi: (bi, 0, 0)),
                   pl.BlockSpec((rows, HEAD_PAIRS, LANES, LANES), lambda bi: (bi, 0, 0, 0))],
        out_shape=[jax.ShapeDtypeStruct((b, t_len, C_VWIDTH), F32),
                   jax.ShapeDtypeStruct((b, HEAD_PAIRS, LANES, LANES), F32)],
        compiler_params=_params(("parallel",)),
        name="hgrn_sample",
    )(zs3, zs3, zs3, zs3, state_t, lb_logits, cnorm3)


def _block_diag(w):
    depth, nb, d, _ = w.shape
    eye = jnp.eye(nb, dtype=w.dtype)
    full = jnp.einsum("lhij,hg->lhigj", w, eye)
    return full.reshape(depth, nb * d, nb * d).astype(BF16)


def _state_to_pairs(st):
    lead = st.shape[:-3]
    t = jnp.swapaxes(st, -1, -2)
    return t.reshape(lead + (HEAD_PAIRS, 2 * C_VDIM, C_KDIM))


def _pairs_to_state(sp):
    lead = sp.shape[:-3]
    t = sp.reshape(lead + (C_HEADS, C_VDIM, C_KDIM))
    return jnp.swapaxes(t, -1, -2)


def kernel(x_prompt, x_sample, cache_k_win, cache_v_win, state_conv, state_lru, state_hgrn, norm_g, w_in, conv_w,
           conv_b, lru_w_a, lru_b_a, lru_w_x, lru_b_x, lru_lambda, hgrn_lb_logits, hgrn_norm_g, w_out, final_norm_g):
    bp, sp, _ = x_prompt.shape
    bs, ts, _ = x_sample.shape
    w_buf = cache_k_win.shape[2]
    assert w_buf == MAX_WINDOW and sp <= MAX_WINDOW

    w_in_bf = w_in.astype(BF16)
    w_out_bf = w_out.astype(BF16)
    norm_g3 = norm_g.reshape(DEPTH, 1, D_MODEL)
    final_g2 = final_norm_g.reshape(1, D_MODEL)
    row3 = lambda p: p.reshape(DEPTH, 1, -1)
    lru_w = (conv_w, row3(conv_b), _block_diag(lru_w_a), row3(lru_b_a), _block_diag(lru_w_x), row3(lru_b_x),
             row3(lru_lambda))
    cnorm3 = row3(hgrn_norm_g)
    cache_kt = jnp.transpose(cache_k_win, (0, 1, 3, 4, 2))
    cache_vt = jnp.transpose(cache_v_win, (0, 1, 3, 4, 2))
    state_lru4 = state_lru.reshape(DEPTH, bs, 1, LRU_WIDTH)
    state_t = _state_to_pairs(state_hgrn)
    bias_p = _prompt_bias()
    bias_s = _sample_bias(ts)

    xp = x_prompt.reshape(bp * sp, D_MODEL)
    xs = x_sample.reshape(bs * ts, D_MODEL)
    outs = {k: [] for k in ("ks", "vs", "cp", "cs", "lp", "ls", "hp", "hs")}
    kt_all = jnp.zeros((DEPTH, bp, A_HEADS, HEAD_DIM, sp), F32)
    vt_all = jnp.zeros((DEPTH, bp, A_HEADS, HEAD_DIM, sp), F32)
    for layer in range(DEPTH):
        last = layer == DEPTH - 1
        z = _inproj(xp, norm_g3, w_in_bf, layer)
        z3 = z.reshape(bp, sp, IN_WIDTH)
        mix_a, kt_all, vt_all = _attn_prompt(z3, bias_p, kt_all, vt_all, layer)
        mix_b, h_last = _lru_prompt(z3, lru_w, layer)
        mix_c, st_p = _hgrn_prompt(z3, hgrn_lb_logits, cnorm3, layer)
        xp = _outproj(xp, mix_a.reshape(bp * sp, -1), mix_b.reshape(bp * sp, -1), mix_c.reshape(bp * sp, -1),
                      w_out_bf, final_g2, layer, last)
        outs["cp"].append(z3[:, sp - (CONV_W - 1):, COL_XB:COL_XB + LRU_WIDTH])
        outs["lp"].append(h_last.reshape(bp, LRU_WIDTH))
        outs["hp"].append(_pairs_to_state(st_p))
        zs = _inproj(xs, norm_g3, w_in_bf, layer)
        zs3 = zs.reshape(bs, ts, IN_WIDTH)
        smix_a = _attn_sample(zs3, cache_kt, cache_vt, bias_s, layer)
        smix_b, sh_last = _lru_sample(zs3, state_conv, state_lru4, lru_w, layer)
        smix_c, st_s = _hgrn_sample(zs3, state_t, hgrn_lb_logits, cnorm3, layer)
        xs = _outproj(xs, smix_a.reshape(bs * ts, -1), smix_b.reshape(bs * ts, -1), smix_c.reshape(bs * ts, -1),
                      w_out_bf, final_g2, layer, last)
        outs["ks"].append(zs3[:, :, COL_KA:COL_KA + A_WIDTH].reshape(bs, ts, A_HEADS, HEAD_DIM))
        outs["vs"].append(zs3[:, :, COL_VA:COL_VA + A_WIDTH].reshape(bs, ts, A_HEADS, HEAD_DIM))
        outs["cs"].append(zs3[:, ts - (CONV_W - 1):, COL_XB:COL_XB + LRU_WIDTH])
        outs["ls"].append(sh_last.reshape(bs, LRU_WIDTH))
        outs["hs"].append(_pairs_to_state(st_s))
    st = lambda k: jnp.stack(outs[k])
    k_win_p = jnp.transpose(kt_all, (0, 1, 4, 2, 3))
    v_win_p = jnp.transpose(vt_all, (0, 1, 4, 2, 3))
    return (xp.reshape(bp, sp, D_MODEL), xs.reshape(bs, ts, D_MODEL), k_win_p, v_win_p, st("ks"), st("vs"),
            st("cp"), st("cs"), st("lp"), st("ls"), st("hp"), st("hs"))
```

```python
import functools

import jax
import jax.numpy as jnp
import numpy as np
from jax import lax
from jax.experimental import pallas as pl
from jax.experimental.pallas import tpu as pltpu

F32 = jnp.float32
BF16 = jnp.bfloat16

D_MODEL = 1024
DEPTH = 4
HEAD_DIM = 64
A_HEADS = 6
A_WIDTH = A_HEADS * HEAD_DIM
DILATED = ((128, 1), (512, 4), (2048, 16))
MAX_WINDOW = 2048
LRU_BLOCKS = 6
LRU_BLOCK_DIM = 64
LRU_WIDTH = LRU_BLOCKS * LRU_BLOCK_DIM
CONV_W = 4
LRU_C = 8.0
C_HEADS = 6
C_KDIM = 128
C_VDIM = 64
C_KWIDTH = C_HEADS * C_KDIM
C_VWIDTH = C_HEADS * C_VDIM
MIX_WIDTH = A_WIDTH + LRU_WIDTH + C_VWIDTH
IN_WIDTH = 4 * A_WIDTH + 2 * LRU_WIDTH + 2 * C_KWIDTH + 2 * C_VWIDTH
RMS_EPS = 1e-6
LOG2_E = 1.4426950408889634

LANES = 128
SUBLANES = 8
HEAD_PAIRS = A_HEADS // 2
VMEM_LIMIT = 56 * 1024 * 1024

COL_QA, COL_KA, COL_VA, COL_GA = 0, A_WIDTH, 2 * A_WIDTH, 3 * A_WIDTH
COL_XB = 4 * A_WIDTH
COL_GB = COL_XB + LRU_WIDTH
COL_QC = COL_GB + LRU_WIDTH
COL_FC = COL_QC + C_KWIDTH
COL_IC = COL_FC + C_KWIDTH
COL_GC = COL_IC + C_VWIDTH

ATT_TQ = 256
ATT_TK = 512
HGRN_CHUNK = 64
HGRN_GROUP = 256
HGRN_TILE = 512
HGRN_SAMPLE_ROWS = 8
ATT_SAMPLE_ROWS = 2
LRU_TILE = 512
PROJ_TM = 512


def _silu(x):
    return x * jax.nn.sigmoid(x)


def _params(sem, **kw):
    return pltpu.CompilerParams(dimension_semantics=sem, vmem_limit_bytes=VMEM_LIMIT, **kw)


def _inproj_kernel(x_ref, g_ref, w_ref, z_ref, *, n_chunk):
    x = x_ref[...]
    h = (x * lax.rsqrt(jnp.mean(x * x, axis=-1, keepdims=True) + RMS_EPS) * g_ref[...]).astype(BF16)
    for n in range(IN_WIDTH // n_chunk):
        sl = slice(n * n_chunk, (n + 1) * n_chunk)
        z_ref[:, sl] = jnp.dot(h, w_ref[:, sl], preferred_element_type=F32)


def _inproj(x2d, norm_g3, w_in_bf, layer):
    m = x2d.shape[0]
    tm = min(PROJ_TM, m)
    assert m % tm == 0
    return pl.pallas_call(
        functools.partial(_inproj_kernel, n_chunk=512),
        grid=(m // tm,),
        in_specs=[
            pl.BlockSpec((tm, D_MODEL), lambda i: (i, 0)),
            pl.BlockSpec((None, 1, D_MODEL), lambda i: (layer, 0, 0)),
            pl.BlockSpec((None, D_MODEL, IN_WIDTH), lambda i: (layer, 0, 0)),
        ],
        out_specs=pl.BlockSpec((tm, IN_WIDTH), lambda i: (i, 0)),
        out_shape=jax.ShapeDtypeStruct((m, IN_WIDTH), F32),
        compiler_params=_params(("parallel",)),
        name="inproj",
    )(x2d, norm_g3, w_in_bf)


def _outproj_kernel(x_ref, ma_ref, mb_ref, mc_ref, w_ref, g_ref, y_ref, *, final_norm):
    y = x_ref[...]
    y = y + jnp.dot(ma_ref[...].astype(BF16), w_ref[0:A_WIDTH, :], preferred_element_type=F32)
    y = y + jnp.dot(mb_ref[...].astype(BF16), w_ref[A_WIDTH:A_WIDTH + LRU_WIDTH, :], preferred_element_type=F32)
    y = y + jnp.dot(mc_ref[...].astype(BF16), w_ref[A_WIDTH + LRU_WIDTH:, :], preferred_element_type=F32)
    if final_norm:
        y = y * lax.rsqrt(jnp.mean(y * y, axis=-1, keepdims=True) + RMS_EPS) * g_ref[...]
    y_ref[...] = y


def _outproj(x2d, mix_a, mix_b, mix_c, w_out_bf, final_g2, layer, final_norm):
    m = x2d.shape[0]
    tm = min(PROJ_TM, m)
    assert m % tm == 0
    mspec = lambda w: pl.BlockSpec((tm, w), lambda i: (i, 0))
    return pl.pallas_call(
        functools.partial(_outproj_kernel, final_norm=final_norm),
        grid=(m // tm,),
        in_specs=[
            mspec(D_MODEL), mspec(A_WIDTH), mspec(LRU_WIDTH), mspec(C_VWIDTH),
            pl.BlockSpec((None, MIX_WIDTH, D_MODEL), lambda i: (layer, 0, 0)),
            pl.BlockSpec((1, D_MODEL), lambda i: (0, 0)),
        ],
        out_specs=mspec(D_MODEL),
        out_shape=jax.ShapeDtypeStruct((m, D_MODEL), F32),
        compiler_params=_params(("parallel",)),
        name="outproj",
    )(x2d, mix_a, mix_b, mix_c, w_out_bf, final_g2)


def _branch_count(delta):
    delta = np.asarray(delta)
    cnt = np.zeros(delta.shape, np.int32)
    for window, dil in DILATED:
        cnt += ((delta >= 0) & (delta <= window) & (delta % dil == 0)).astype(np.int32)
    return cnt


def _log_count(cnt):
    with np.errstate(divide="ignore"):
        return np.where(cnt > 0, np.log(np.maximum(cnt, 1).astype(np.float64)), -np.inf).astype(np.float32)


ATT_FAR_OFF = (DILATED[1][0] + ATT_TK - 1) // ATT_TQ + 1


def _prompt_bias():
    qi = np.arange(ATT_TQ)[None, :]
    ki = np.arange(ATT_TK)[:, None]
    tabs = []
    for off in range(ATT_FAR_OFF + 1):
        b = _log_count(_branch_count(off * ATT_TQ + qi - ki)) * np.float32(LOG2_E)
        tabs.append(np.concatenate([b, b], axis=1))
    return jnp.asarray(np.stack(tabs))


def _attn_pair_kernel(qa_ref, qb_ref, k_ref, v_ref, ga_ref, gb_ref, bias_ref, kt_in_ref, vt_in_ref,
                      oa_ref, ob_ref, kt_ref, vt_ref, *, n_q):
    t = pl.program_id(2)
    lane = lax.broadcasted_iota(jnp.int32, (1, LANES), 1)
    head_a = lane < HEAD_DIM
    ratio = ATT_TK // ATT_TQ
    nt = (((1,), (1,)), ((), ()))
    tn = (((0,), (0,)), ((), ()))

    def stacked(q_ref):
        q = q_ref[...] * (HEAD_DIM ** -0.5 * LOG2_E)
        return jnp.concatenate([jnp.where(head_a, q, 0.0), jnp.where(head_a, 0.0, q)], axis=0).astype(BF16)

    def step(q2, i, j, carry):
        m, l, acc = carry
        k0, k1 = j * ATT_TK, min((j + 1) * ATT_TK, (i + 1) * ATT_TQ)
        kb = k_ref[k0:k1, :].astype(BF16)
        vb = v_ref[k0:k1, :].astype(BF16)
        bias = bias_ref[min(i - j * ratio, ATT_FAR_OFF)][:k1 - k0]
        s = lax.dot_general(kb, q2, nt, preferred_element_type=F32) + bias
        m_new = jnp.maximum(m, jnp.max(s, axis=0, keepdims=True))
        alpha = jnp.exp2(m - m_new)
        p = jnp.exp2(s - m_new)
        l = alpha * l + jnp.sum(p, axis=0, keepdims=True)
        pv = lax.dot_general(vb, p.astype(BF16), tn, preferred_element_type=F32)
        return m_new, l, alpha * acc + pv

    def finish(carry, g_ref):
        _, l, acc = carry
        o2 = (acc / l).T
        o = jnp.where(head_a, o2[:ATT_TQ], o2[ATT_TQ:])
        return (o * _silu(g_ref[...])).astype(BF16)

    def run(tt):
        i_a, i_b = tt, n_q - 1 - tt
        n_a, n_b = (i_a * ATT_TQ) // ATT_TK + 1, (i_b * ATT_TQ) // ATT_TK + 1
        q2a, q2b = stacked(qa_ref), stacked(qb_ref)
        init = (jnp.full((1, 2 * ATT_TQ), -jnp.inf, F32), jnp.zeros((1, 2 * ATT_TQ), F32),
                jnp.zeros((LANES, 2 * ATT_TQ), F32))
        ca, cb = init, init
        for j in range(max(n_a, n_b)):
            if j < n_b:
                cb = step(q2b, i_b, j, cb)
            if j < n_a:
                ca = step(q2a, i_a, j, ca)
        return finish(ca, ga_ref), finish(cb, gb_ref)

    oa, ob = lax.switch(t, [functools.partial(run, tt) for tt in range(n_q // 2)])
    oa_ref[...] = oa
    ob_ref[...] = ob

    @pl.when(t == 0)
    def _():
        s_len = k_ref.shape[0]
        for src, dst in ((k_ref, kt_ref), (v_ref, vt_ref)):
            for c in range(s_len // LANES):
                blk = src[c * LANES:(c + 1) * LANES, :].T
                dst[:, :, c * LANES:(c + 1) * LANES] = blk.reshape(2, HEAD_DIM, LANES)


def _attn_prompt(z3, bias, kt_all, vt_all, layer):
    b, s, _ = z3.shape
    n_q = s // ATT_TQ
    assert ATT_TK % ATT_TQ == 0 and s % ATT_TK == 0 and s <= MAX_WINDOW and ATT_TQ % DILATED[2][1] == 0
    assert n_q % 2 == 0
    half = n_q // 2
    col = lambda base: (lambda bi, p, t: (bi, 0, base // LANES + p))
    fwd = lambda base: pl.BlockSpec((None, ATT_TQ, LANES), lambda bi, p, t: (bi, t, base // LANES + p))
    rev = lambda base: pl.BlockSpec((None, ATT_TQ, LANES), lambda bi, p, t: (bi, n_q - 1 - t, base // LANES + p))
    win = pl.BlockSpec((None, None, 2, HEAD_DIM, s), lambda bi, p, t: (layer, bi, p, 0, 0))
    half_out = pl.BlockSpec((None, ATT_TQ, LANES), lambda bi, p, t: (bi, t, p))
    o_lo, o_hi, kt_all, vt_all = pl.pallas_call(
        functools.partial(_attn_pair_kernel, n_q=n_q),
        grid=(b, HEAD_PAIRS, half),
        in_specs=[
            fwd(COL_QA), rev(COL_QA),
            pl.BlockSpec((None, s, LANES), col(COL_KA)),
            pl.BlockSpec((None, s, LANES), col(COL_VA)),
            fwd(COL_GA), rev(COL_GA),
            pl.BlockSpec((ATT_FAR_OFF + 1, ATT_TK, 2 * ATT_TQ), lambda bi, p, t: (0, 0, 0)),
            pl.BlockSpec(memory_space=pl.ANY),
            pl.BlockSpec(memory_space=pl.ANY),
        ],
        out_specs=[half_out, half_out, win, win],
        out_shape=[jax.ShapeDtypeStruct((b, half * ATT_TQ, A_WIDTH), BF16),
                   jax.ShapeDtypeStruct((b, half * ATT_TQ, A_WIDTH), BF16),
                   jax.ShapeDtypeStruct(kt_all.shape, F32), jax.ShapeDtypeStruct(vt_all.shape, F32)],
        input_output_aliases={7: 2, 8: 3},
        compiler_params=_params(("parallel", "parallel", "arbitrary")),
        name="attn_prompt",
    )(z3, z3, z3, z3, z3, z3, bias, kt_all, vt_all)
    o_hi = o_hi.reshape(b, half, ATT_TQ, A_WIDTH)[:, ::-1].reshape(b, half * ATT_TQ, A_WIDTH)
    return jnp.concatenate([o_lo, o_hi], axis=1), kt_all, vt_all


def _sample_bias(t_len):
    qpos = MAX_WINDOW + np.arange(t_len)[:, None]
    bias_c = _log_count(_branch_count(qpos - np.arange(MAX_WINDOW)[None, :]))
    bias_n = _log_count(_branch_count(qpos - (MAX_WINDOW + np.arange(t_len)[None, :])))
    return jnp.asarray(bias_c), jnp.asarray(bias_n)


def _split_heads(x):
    return jnp.stack([x[:, h * HEAD_DIM:(h + 1) * HEAD_DIM] for h in range(A_HEADS)], axis=0)


def _attn_sample_row(q, k, v, g, kt, vt, bias_c, bias_n):
    q3 = _split_heads(q * (HEAD_DIM ** -0.5))
    kn3 = _split_heads(k)
    vn3 = _split_heads(v)
    bdot = lambda a, b, ca, cb: lax.dot_general(a, b, (((ca,), (cb,)), ((0,), (0,))), preferred_element_type=F32)
    s_c = bdot(q3, kt, 2, 1) + bias_c[None]
    s_n = bdot(q3, kn3, 2, 2) + bias_n[None]
    m = jnp.maximum(jnp.max(s_c, axis=-1, keepdims=True), jnp.max(s_n, axis=-1, keepdims=True))
    p_c = jnp.exp(s_c - m)
    p_n = jnp.exp(s_n - m)
    l = jnp.sum(p_c, axis=-1, keepdims=True) + jnp.sum(p_n, axis=-1, keepdims=True)
    acc = bdot(p_c, vt, 2, 2) + bdot(p_n, vn3, 2, 1)
    o3 = acc / l
    o = jnp.concatenate([o3[h] for h in range(A_HEADS)], axis=-1)
    return o * _silu(g)


def _neg_expm1(x):
    series = -x * (1.0 + x * (0.5 + x * (1.0 / 6.0 + x * (1.0 / 24.0 + x * (1.0 / 120.0)))))
    return jnp.where(x > -0.1, series, 1.0 - jnp.exp(x))


def _softplus(y):
    return jnp.maximum(y, 0.0) + jnp.log1p(jnp.exp(-jnp.abs(y)))


def _lru_gates(xc, wa_ref, ba_ref, wx_ref, bx_ref, lam_ref):
    xcb = xc.astype(BF16)
    r = jax.nn.sigmoid(jnp.dot(xcb, wa_ref[...], preferred_element_type=F32) + ba_ref[...])
    i = jax.nn.sigmoid(jnp.dot(xcb, wx_ref[...], preferred_element_type=F32) + bx_ref[...])
    log_a = -LRU_C * r * _softplus(-lam_ref[...])
    a = jnp.exp(log_a)
    u = jnp.sqrt(_neg_expm1(2.0 * log_a)) * (i * xc)
    return a, u


def _scan8(a, u, axis, row):
    for sh in (1, 2, 4):
        keep = row >= sh
        a_prev = jnp.where(keep, pltpu.roll(a, sh, axis), 1.0)
        u_prev = jnp.where(keep, pltpu.roll(u, sh, axis), 0.0)
        u = a * u_prev + u
        a = a * a_prev
    return a, u


def _lru_prompt_kernel(xb_ref, gb_ref, cw_ref, cb_ref, wa_ref, ba_ref, wx_ref, bx_ref, lam_ref,
                       o_ref, hlast_ref, xp_scr, a_scr, u_scr, hc_scr, *, tile):
    t = pl.program_id(1)

    @pl.when(t == 0)
    def _():
        xp_scr[0:SUBLANES, :] = jnp.zeros((SUBLANES, LRU_WIDTH), F32)
        hc_scr[...] = jnp.zeros_like(hc_scr)

    xp_scr[SUBLANES:SUBLANES + tile, :] = xb_ref[...]
    cw = cw_ref[...]
    xc = cb_ref[...]
    for j in range(CONV_W):
        off = SUBLANES - (CONV_W - 1) + j
        xc = xc + xp_scr[off:off + tile, :] * cw[j:j + 1, :]
    xp_scr[0:SUBLANES, :] = xp_scr[tile:tile + SUBLANES, :]

    a, u = _lru_gates(xc, wa_ref, ba_ref, wx_ref, bx_ref, lam_ref)
    row = lax.broadcasted_iota(jnp.int32, (1, SUBLANES, 1), 1)
    grouped = (tile // SUBLANES, SUBLANES, LRU_WIDTH)
    a, u = _scan8(a.reshape(grouped), u.reshape(grouped), 1, row)
    a_scr[...] = a.reshape(tile, LRU_WIDTH)
    u_scr[...] = u.reshape(tile, LRU_WIDTH)

    def body(gi, hc):
        r0 = pl.multiple_of(gi * SUBLANES, SUBLANES)
        h = a_scr[pl.ds(r0, SUBLANES), :] * hc + u_scr[pl.ds(r0, SUBLANES), :]
        u_scr[pl.ds(r0, SUBLANES), :] = h
        return jnp.broadcast_to(h[SUBLANES - 1:SUBLANES, :], (SUBLANES, LRU_WIDTH))

    hc = lax.fori_loop(0, tile // SUBLANES, body, hc_scr[...])
    hc_scr[...] = hc
    o_ref[...] = (u_scr[...] * _silu(gb_ref[...])).astype(o_ref.dtype)

    @pl.when(t == pl.num_programs(1) - 1)
    def _():
        hlast_ref[...] = hc[0:1, :]


def _lru_weight_specs(layer, nargs):
    def spec(shape):
        idx = (layer,) + (0,) * len(shape)
        if nargs == 1:
            return pl.BlockSpec((None,) + shape, lambda a: idx)
        return pl.BlockSpec((None,) + shape, lambda a, b: idx)
    return [spec((CONV_W, LRU_WIDTH)), spec((1, LRU_WIDTH)),
            spec((LRU_WIDTH, LRU_WIDTH)), spec((1, LRU_WIDTH)),
            spec((LRU_WIDTH, LRU_WIDTH)), spec((1, LRU_WIDTH)), spec((1, LRU_WIDTH))]


def _lru_prompt(z3, lru_w, layer):
    b, s, _ = z3.shape
    tile = min(LRU_TILE, s)
    assert s % tile == 0
    return pl.pallas_call(
        functools.partial(_lru_prompt_kernel, tile=tile),
        grid=(b, s // tile),
        in_specs=[pl.BlockSpec((None, tile, LRU_WIDTH), lambda bi, t: (bi, t, COL_XB // LRU_WIDTH)),
                  pl.BlockSpec((None, tile, LRU_WIDTH), lambda bi, t: (bi, t, COL_GB // LRU_WIDTH))]
        + _lru_weight_specs(layer, 2),
        out_specs=[pl.BlockSpec((None, tile, LRU_WIDTH), lambda bi, t: (bi, t, 0)),
                   pl.BlockSpec((None, 1, LRU_WIDTH), lambda bi, t: (bi, 0, 0))],
        out_shape=[jax.ShapeDtypeStruct((b, s, LRU_WIDTH), BF16),
                   jax.ShapeDtypeStruct((b, 1, LRU_WIDTH), F32)],
        scratch_shapes=[pltpu.VMEM((tile + 2 * SUBLANES, LRU_WIDTH), F32),
                        pltpu.VMEM((tile, LRU_WIDTH), F32),
                        pltpu.VMEM((tile, LRU_WIDTH), F32),
                        pltpu.VMEM((SUBLANES, LRU_WIDTH), F32)],
        compiler_params=_params(("parallel", "arbitrary")),
        name="lru_prompt",
    )(z3, z3, *lru_w)


def _lru_sample_kernel(xb_ref, gb_ref, cs_ref, h0_ref, cw_ref, cb_ref, wa_ref, ba_ref, wx_ref, bx_ref, lam_ref,
                       o_ref, hlast_ref):
    x = xb_ref[...]
    nb, t_len, _ = x.shape
    cs = cs_ref[...]
    row = lax.broadcasted_iota(jnp.int32, (1, t_len, 1), 1)
    cw = cw_ref[...]
    xc = cb_ref[...] + x * cw[CONV_W - 1:CONV_W, :]
    for back in range(1, CONV_W):
        xs = pltpu.roll(x, back, 1)
        for r in range(back):
            src = CONV_W - 1 - back + r
            xs = jnp.where(row == r, cs[:, src:src + 1, :], xs)
        xc = xc + xs * cw[CONV_W - 1 - back:CONV_W - back, :]
    a, u = _lru_gates(xc.reshape(nb * t_len, LRU_WIDTH), wa_ref, ba_ref, wx_ref, bx_ref, lam_ref)
    a = a.reshape(nb, t_len, LRU_WIDTH)
    u = u.reshape(nb, t_len, LRU_WIDTH)
    a, u = _scan8(a, u, 1, row)
    h = a * h0_ref[...] + u
    o_ref[...] = h * _silu(gb_ref[...])
    hlast_ref[...] = h[:, t_len - 1:t_len, :]


def _lru_sample(zs3, state_conv, state_lru4, lru_w, layer):
    b, t_len, _ = zs3.shape
    assert t_len == SUBLANES
    return pl.pallas_call(
        _lru_sample_kernel,
        grid=(1,),
        in_specs=[pl.BlockSpec((b, t_len, LRU_WIDTH), lambda i: (0, 0, COL_XB // LRU_WIDTH)),
                  pl.BlockSpec((b, t_len, LRU_WIDTH), lambda i: (0, 0, COL_GB // LRU_WIDTH)),
                  pl.BlockSpec((None, b, CONV_W - 1, LRU_WIDTH), lambda i: (layer, 0, 0, 0)),
                  pl.BlockSpec((None, b, 1, LRU_WIDTH), lambda i: (layer, 0, 0, 0))]
        + _lru_weight_specs(layer, 1),
        out_specs=[pl.BlockSpec((b, t_len, LRU_WIDTH), lambda i: (0, 0, 0)),
                   pl.BlockSpec((b, 1, LRU_WIDTH), lambda i: (0, 0, 0))],
        out_shape=[jax.ShapeDtypeStruct((b, t_len, LRU_WIDTH), F32),
                   jax.ShapeDtypeStruct((b, 1, LRU_WIDTH), F32)],
        compiler_params=_params(("arbitrary",)),
        name="lru_sample",
    )(zs3, zs3, state_conv, state_lru4, *lru_w)


def _split3(x):
    hi = x.astype(BF16)
    r = x - hi.astype(F32)
    mid = r.astype(BF16)
    lo = (r - mid.astype(F32)).astype(BF16)
    return hi, mid, lo


def _pad_rows(x, n):
    if x.shape[0] == n:
        return x
    return jnp.concatenate([x, jnp.zeros((n - x.shape[0], x.shape[1]), x.dtype)], axis=0)


def _lower_bound(logits, layer):
    e = jnp.exp(logits - jnp.max(logits, axis=0, keepdims=True))
    den = jnp.sum(e, axis=0, keepdims=True)
    num = jnp.zeros_like(den)
    for r in range(1, layer + 1):
        num = num + e[r:r + 1, :]
    return num / den


def _chunk_cumsum(x, chunk):
    n = x.shape[0]
    if chunk == SUBLANES:
        x3 = x.reshape(n // SUBLANES, SUBLANES, x.shape[1])
        sub = lax.broadcasted_iota(jnp.int32, (1, SUBLANES, 1), 1)
        for sh in (1, 2, 4):
            x3 = x3 + jnp.where(sub >= sh, pltpu.roll(x3, sh, 1), 0.0)
        return x3.reshape(n, x.shape[1])
    row = lax.broadcasted_iota(jnp.int32, (n, 1), 0)
    col = lax.broadcasted_iota(jnp.int32, (1, n), 1)
    tri = ((row >= col) & (row // chunk == col // chunk)).astype(BF16)
    out = None
    for part in _split3(x):
        term = jnp.dot(tri, part, preferred_element_type=F32)
        out = term if out is None else out + term
    return out


def _hgrn_tile(qc2, fc2, v, gc, st, lb2, cg, chunk, carry=True):
    n = v.shape[0]
    nc = n // chunk
    nblk = chunk // SUBLANES
    lane = lax.broadcasted_iota(jnp.int32, (1, LANES), 1)
    head_a = lane < C_VDIM
    sub = lax.broadcasted_iota(jnp.int32, (1, SUBLANES, 1), 1)
    blocked = lambda x: x.reshape(n // SUBLANES, SUBLANES, LANES)
    chunked = lambda x: x.reshape(nc, chunk, LANES)
    bnt = (((2,), (2,)), ((0,), (0,)))
    bnn = (((2,), (1,)), ((0,), (0,)))
    nt = (((1,), (1,)), ((), ()))

    q2 = _silu(qc2) * (C_KDIM ** -0.5)
    g2 = lb2 + (1.0 - lb2) * jax.nn.sigmoid(fc2)
    k2 = 1.0 - g2
    b2 = _chunk_cumsum(jnp.log(g2) * LOG2_E, chunk)
    v3 = blocked(v)
    vc = chunked(v).astype(BF16)
    local, q_states, decays, k_ends = [], [], [], []
    for a in range(2):
        hs = slice(a * C_KDIM, (a + 1) * C_KDIM)
        q, k, b = q2[:, hs], k2[:, hs], b2[:, hs]
        q3, k3, g3 = blocked(q), blocked(k), blocked(g2[:, hs])
        o3 = jnp.sum(q3 * k3, axis=-1, keepdims=True) * v3
        decay = g3
        for d in range(1, SUBLANES):
            g_back = pltpu.roll(g3, d, 1)
            w = q3 * (1.0 - g_back) * decay
            a_d = jnp.where(sub >= d, jnp.sum(w, axis=-1, keepdims=True), 0.0)
            o3 = o3 + a_d * pltpu.roll(v3, d, 1)
            decay = decay * g_back
        o = o3.reshape(n, LANES)
        qc_, kc_, bc_ = chunked(q), chunked(k), chunked(b)
        if nblk > 1:
            blocks = [jnp.zeros((nc, SUBLANES, chunk), F32)]
            for i in range(1, nblk):
                lo = i * SUBLANES
                ref = bc_[:, lo - 1:lo, :]
                kt = kc_[:, :lo, :] * jnp.exp2(ref - bc_[:, :lo, :])
                kt = jnp.concatenate([kt, jnp.zeros((nc, chunk - lo, LANES), F32)], axis=1).astype(BF16)
                qt = (qc_[:, lo:lo + SUBLANES, :] * jnp.exp2(bc_[:, lo:lo + SUBLANES, :] - ref)).astype(BF16)
                blocks.append(lax.dot_general(qt, kt, bnt, preferred_element_type=F32))
            attn = jnp.concatenate(blocks, axis=1).astype(BF16)
            o = o + lax.dot_general(attn, vc, bnn, preferred_element_type=F32).reshape(n, LANES)
        local.append(o)
        last = bc_[:, chunk - 1:chunk, :]
        decays.append(jnp.exp2(last))
        k_ends.append(kc_ * jnp.exp2(last - bc_))
        q_states.append((qc_ * jnp.exp2(bc_)).astype(BF16))
    top = lax.broadcasted_iota(jnp.int32, (LANES, 1), 0) < C_VDIM
    pad = lambda x: _pad_rows(x, LANES).astype(BF16)
    inter = [[], []]
    new_states = []
    for c in range(nc):
        st_c = st if carry else st[c]
        st_b = st_c.astype(BF16)
        for a in range(2):
            inter[a].append(lax.dot_general(q_states[a][c], st_b, nt, preferred_element_type=F32))
        v_t = _pad_rows(v[c * chunk:(c + 1) * chunk], LANES).T.astype(BF16)
        upd_a = jnp.dot(v_t, pad(k_ends[0][c]), preferred_element_type=F32)
        upd_b = jnp.dot(v_t, pad(k_ends[1][c]), preferred_element_type=F32)
        st_c = jnp.where(top, st_c * decays[0][c] + upd_a, st_c * decays[1][c] + upd_b)
        if carry:
            st = st_c
        else:
            new_states.append(st_c)
    if not carry:
        st = jnp.stack(new_states)
    cat = lambda xs: xs[0] if len(xs) == 1 else jnp.concatenate(xs, axis=0)
    o = jnp.where(head_a, local[0] + cat(inter[0]), local[1] + cat(inter[1]))
    sq = o * o
    ms_a = jnp.sum(jnp.where(head_a, sq, 0.0), axis=-1, keepdims=True) * (1.0 / C_VDIM)
    ms_b = jnp.sum(jnp.where(head_a, 0.0, sq), axis=-1, keepdims=True) * (1.0 / C_VDIM)
    inv = jnp.where(head_a, lax.rsqrt(ms_a + RMS_EPS), lax.rsqrt(ms_b + RMS_EPS))
    return o * inv * cg * _silu(gc), st


def _hgrn_prompt_kernel(qc_ref, fc_ref, ic_ref, gc_ref, lbl_ref, cg_ref,
                        sq_ref, sk_ref, sv_ref, sg_ref, kt_ref, vt_ref, bias_c_ref, bias_n_ref,
                        o_ref, st_ref, so_ref, st_scr, *, layer, tile, row_steps):
    t = pl.program_id(2)
    step = (pl.program_id(0) * pl.num_programs(1) + pl.program_id(1)) * pl.num_programs(2) + t

    @pl.when(step < row_steps)
    def _():
        for r in range(sq_ref.shape[0]):
            so_ref[r] = _attn_sample_row(sq_ref[r], sk_ref[r], sv_ref[r], sg_ref[r], kt_ref[r], vt_ref[r],
                                         bias_c_ref[...], bias_n_ref[...])

    @pl.when(t == 0)
    def _():
        st_scr[...] = jnp.zeros_like(st_scr)

    lb2 = _lower_bound(lbl_ref[...], layer)
    cg = cg_ref[...]

    def body(c, carry):
        r0 = pl.multiple_of(c * HGRN_GROUP, HGRN_GROUP)
        rows = pl.ds(r0, HGRN_GROUP)
        out, st_new = _hgrn_tile(qc_ref[rows, :], fc_ref[rows, :], ic_ref[rows, :], gc_ref[rows, :],
                                 st_scr[...], lb2, cg, HGRN_CHUNK)
        o_ref[rows, :] = out.astype(o_ref.dtype)
        st_scr[...] = st_new
        return carry

    lax.fori_loop(0, tile // HGRN_GROUP, body, 0)

    @pl.when(t == pl.num_programs(2) - 1)
    def _():
        st_ref[...] = st_scr[...]


def _hgrn_prompt(z3, lb_logits, cnorm3, zs3, cache_kt, cache_vt, bias_s, layer):
    b, s, _ = z3.shape
    bs, t_len, _ = zs3.shape
    w_buf = cache_kt.shape[-1]
    tile = min(HGRN_TILE, s)
    assert s % tile == 0 and tile % HGRN_CHUNK == 0
    kw, vw = 2 * C_KDIM, 2 * C_VDIM
    n_t = s // tile
    rows = min(ATT_SAMPLE_ROWS, bs)
    row_steps = bs // rows
    assert bs % rows == 0 and row_steps <= b * HEAD_PAIRS * n_t
    grp = lambda bi, p, t: jnp.minimum((bi * HEAD_PAIRS + p) * n_t + t, row_steps - 1)
    zcol = lambda base: pl.BlockSpec((rows, t_len, A_WIDTH), lambda bi, p, t: (grp(bi, p, t), 0, base // A_WIDTH))
    cache = pl.BlockSpec((None, rows, A_HEADS, HEAD_DIM, w_buf), lambda bi, p, t: (layer, grp(bi, p, t), 0, 0, 0))
    bias_c, bias_n = bias_s
    return pl.pallas_call(
        functools.partial(_hgrn_prompt_kernel, layer=layer, tile=tile, row_steps=row_steps),
        grid=(b, HEAD_PAIRS, n_t),
        in_specs=[pl.BlockSpec((None, tile, kw), lambda bi, p, t: (bi, t, COL_QC // kw + p)),
                  pl.BlockSpec((None, tile, kw), lambda bi, p, t: (bi, t, COL_FC // kw + p)),
                  pl.BlockSpec((None, tile, vw), lambda bi, p, t: (bi, t, COL_IC // vw + p)),
                  pl.BlockSpec((None, tile, vw), lambda bi, p, t: (bi, t, COL_GC // vw + p)),
                  pl.BlockSpec((DEPTH, kw), lambda bi, p, t: (0, p)),
                  pl.BlockSpec((None, 1, vw), lambda bi, p, t: (layer, 0, p)),
                  zcol(COL_QA), zcol(COL_KA), zcol(COL_VA), zcol(COL_GA), cache, cache,
                  pl.BlockSpec(bias_c.shape, lambda bi, p, t: (0, 0)),
                  pl.BlockSpec(bias_n.shape, lambda bi, p, t: (0, 0))],
        out_specs=[pl.BlockSpec((None, tile, vw), lambda bi, p, t: (bi, t, p)),
                   pl.BlockSpec((None, None, LANES, LANES), lambda bi, p, t: (bi, p, 0, 0)),
                   pl.BlockSpec((rows, t_len, A_WIDTH), lambda bi, p, t: (grp(bi, p, t), 0, 0))],
        out_shape=[jax.ShapeDtypeStruct((b, s, C_VWIDTH), BF16),
                   jax.ShapeDtypeStruct((b, HEAD_PAIRS, LANES, LANES), F32),
                   jax.ShapeDtypeStruct((bs, t_len, A_WIDTH), F32)],
        scratch_shapes=[pltpu.VMEM((LANES, LANES), F32)],
        compiler_params=_params(("arbitrary", "arbitrary", "arbitrary")),
        name="hgrn_prompt",
    )(z3, z3, z3, z3, lb_logits, cnorm3, zs3, zs3, zs3, zs3, cache_kt, cache_vt, bias_c, bias_n)


def _hgrn_sample_kernel(qc_ref, fc_ref, ic_ref, gc_ref, st_in_ref, lbl_ref, cg_ref, o_ref, st_ref, *, layer):
    lb = _lower_bound(lbl_ref[...], layer)
    rows, t_len, _ = qc_ref.shape
    kw, vw = 2 * C_KDIM, 2 * C_VDIM
    flat = lambda ref, sl: ref[:, :, sl].reshape(rows * t_len, sl.stop - sl.start)
    for p in range(HEAD_PAIRS):
        ks, vs = slice(p * kw, (p + 1) * kw), slice(p * vw, (p + 1) * vw)
        out, st_new = _hgrn_tile(flat(qc_ref, ks), flat(fc_ref, ks), flat(ic_ref, vs), flat(gc_ref, vs),
                                 st_in_ref[:, p], lb[:, ks], cg_ref[:, vs], t_len, carry=False)
        o_ref[:, :, vs] = out.reshape(rows, t_len, vw)
        st_ref[:, p] = st_new


def _hgrn_sample(zs3, state_t, lb_logits, cnorm3, layer):
    b, t_len, _ = zs3.shape
    rows = min(HGRN_SAMPLE_ROWS, b)
    assert t_len == SUBLANES and b % rows == 0
    return pl.pallas_call(
        functools.partial(_hgrn_sample_kernel, layer=layer),
        grid=(b // rows,),
        in_specs=[pl.BlockSpec((rows, t_len, C_KWIDTH), lambda bi: (bi, 0, COL_QC // C_KWIDTH)),
                  pl.BlockSpec((rows, t_len, C_KWIDTH), lambda bi: (bi, 0, COL_FC // C_KWIDTH)),
                  pl.BlockSpec((rows, t_len, C_VWIDTH), lambda bi: (bi, 0, COL_IC // C_VWIDTH)),
                  pl.BlockSpec((rows, t_len, C_VWIDTH), lambda bi: (bi, 0, COL_GC // C_VWIDTH)),
                  pl.BlockSpec((None, rows, HEAD_PAIRS, LANES, LANES), lambda bi: (layer, bi, 0, 0, 0)),
                  pl.BlockSpec((DEPTH, C_KWIDTH), lambda bi: (0, 0)),
                  pl.BlockSpec((None, 1, C_VWIDTH), lambda bi: (layer, 0, 0))],
        out_specs=[pl.BlockSpec((rows, t_len, C_VWIDTH), lambda bi: (bi, 0, 0)),
                   pl.BlockSpec((rows, HEAD_PAIRS, LANES, LANES), lambda bi: (bi, 0, 0, 0))],
        out_shape=[jax.ShapeDtypeStruct((b, t_len, C_VWIDTH), F32),
                   jax.ShapeDtypeStruct((b, HEAD_PAIRS, LANES, LANES), F32)],
        compiler_params=_params(("parallel",)),
        name="hgrn_sample",
    )(zs3, zs3, zs3, zs3, state_t, lb_logits, cnorm3)


def _block_diag(w):
    depth, nb, d, _ = w.shape
    eye = jnp.eye(nb, dtype=w.dtype)
    full = jnp.einsum("lhij,hg->lhigj", w, eye)
    return full.reshape(depth, nb * d, nb * d).astype(BF16)


def _state_to_pairs(st):
    lead = st.shape[:-3]
    t = jnp.swapaxes(st, -1, -2)
    return t.reshape(lead + (HEAD_PAIRS, 2 * C_VDIM, C_KDIM))


def _pairs_to_state(sp):
    lead = sp.shape[:-3]
    t = sp.reshape(lead + (C_HEADS, C_VDIM, C_KDIM))
    return jnp.swapaxes(t, -1, -2)


def kernel(x_prompt, x_sample, cache_k_win, cache_v_win, state_conv, state_lru, state_hgrn, norm_g, w_in, conv_w,
           conv_b, lru_w_a, lru_b_a, lru_w_x, lru_b_x, lru_lambda, hgrn_lb_logits, hgrn_norm_g, w_out, final_norm_g):
    bp, sp, _ = x_prompt.shape
    bs, ts, _ = x_sample.shape
    w_buf = cache_k_win.shape[2]
    assert w_buf == MAX_WINDOW and sp <= MAX_WINDOW

    w_in_bf = w_in.astype(BF16)
    w_out_bf = w_out.astype(BF16)
    norm_g3 = norm_g.reshape(DEPTH, 1, D_MODEL)
    final_g2 = final_norm_g.reshape(1, D_MODEL)
    row3 = lambda p: p.reshape(DEPTH, 1, -1)
    lru_w = (conv_w, row3(conv_b), _block_diag(lru_w_a), row3(lru_b_a), _block_diag(lru_w_x), row3(lru_b_x),
             row3(lru_lambda))
    cnorm3 = row3(hgrn_norm_g)
    cache_kt = jnp.transpose(cache_k_win, (0, 1, 3, 4, 2))
    cache_vt = jnp.transpose(cache_v_win, (0, 1, 3, 4, 2))
    state_lru4 = state_lru.reshape(DEPTH, bs, 1, LRU_WIDTH)
    state_t = _state_to_pairs(state_hgrn)
    bias_p = _prompt_bias()
    bias_s = _sample_bias(ts)

    xp = x_prompt.reshape(bp * sp, D_MODEL)
    xs = x_sample.reshape(bs * ts, D_MODEL)
    outs = {k: [] for k in ("ks", "vs", "cp", "cs", "lp", "ls", "hp", "hs")}
    kt_all = jnp.zeros((DEPTH, bp, A_HEADS, HEAD_DIM, sp), F32)
    vt_all = jnp.zeros((DEPTH, bp, A_HEADS, HEAD_DIM, sp), F32)
    for layer in range(DEPTH):
        last = layer == DEPTH - 1
        zs = _inproj(xs, norm_g3, w_in_bf, layer)
        zs3 = zs.reshape(bs, ts, IN_WIDTH)
        z = _inproj(xp, norm_g3, w_in_bf, layer)
        z3 = z.reshape(bp, sp, IN_WIDTH)
        mix_a, kt_all, vt_all = _attn_prompt(z3, bias_p, kt_all, vt_all, layer)
        mix_b, h_last = _lru_prompt(z3, lru_w, layer)
        mix_c, st_p, smix_a = _hgrn_prompt(z3, hgrn_lb_logits, cnorm3, zs3, cache_kt, cache_vt, bias_s, layer)
        xp = _outproj(xp, mix_a.reshape(bp * sp, -1), mix_b.reshape(bp * sp, -1), mix_c.reshape(bp * sp, -1),
                      w_out_bf, final_g2, layer, last)
        outs["cp"].append(z3[:, sp - (CONV_W - 1):, COL_XB:COL_XB + LRU_WIDTH])
        outs["lp"].append(h_last.reshape(bp, LRU_WIDTH))
        outs["hp"].append(_pairs_to_state(st_p))
        smix_b, sh_last = _lru_sample(zs3, state_conv, state_lru4, lru_w, layer)
        smix_c, st_s = _hgrn_sample(zs3, state_t, hgrn_lb_logits, cnorm3, layer)
        xs = _outproj(xs, smix_a.reshape(bs * ts, -1), smix_b.reshape(bs * ts, -1), smix_c.reshape(bs * ts, -1),
                      w_out_bf, final_g2, layer, last)
        outs["ks"].append(zs3[:, :, COL_KA:COL_KA + A_WIDTH].reshape(bs, ts, A_HEADS, HEAD_DIM))
        outs["vs"].append(zs3[:, :, COL_VA:COL_VA + A_WIDTH].reshape(bs, ts, A_HEADS, HEAD_DIM))
        outs["cs"].append(zs3[:, ts - (CONV_W - 1):, COL_XB:COL_XB + LRU_WIDTH])
        outs["ls"].append(sh_last.reshape(bs, LRU_WIDTH))
        outs["hs"].append(_pairs_to_state(st_s))
    st = lambda k: jnp.stack(outs[k])
    k_win_p = jnp.transpose(kt_all, (0, 1, 4, 2, 3))
    v_win_p = jnp.transpose(vt_all, (0, 1, 4, 2, 3))
    return (xp.reshape(bp, sp, D_MODEL), xs.reshape(bs, ts, D_MODEL), k_win_p, v_win_p, st("ks"), st("vs"),
            st("cp"), st("cs"), st("lp"), st("ls"), st("hp"), st("hs"))
```

```python
import functools

import jax
import jax.numpy as jnp
import numpy as np
from jax import lax
from jax.experimental import pallas as pl
from jax.experimental.pallas import tpu as pltpu

F32 = jnp.float32
BF16 = jnp.bfloat16

D_MODEL = 1024
DEPTH = 4
HEAD_DIM = 64
A_HEADS = 6
A_WIDTH = A_HEADS * HEAD_DIM
DILATED = ((128, 1), (512, 4), (2048, 16))
MAX_WINDOW = 2048
LRU_BLOCKS = 6
LRU_BLOCK_DIM = 64
LRU_WIDTH = LRU_BLOCKS * LRU_BLOCK_DIM
CONV_W = 4
LRU_C = 8.0
C_HEADS = 6
C_KDIM = 128
C_VDIM = 64
C_KWIDTH = C_HEADS * C_KDIM
C_VWIDTH = C_HEADS * C_VDIM
MIX_WIDTH = A_WIDTH + LRU_WIDTH + C_VWIDTH
IN_WIDTH = 4 * A_WIDTH + 2 * LRU_WIDTH + 2 * C_KWIDTH + 2 * C_VWIDTH
RMS_EPS = 1e-6
LOG2_E = 1.4426950408889634

LANES = 128
SUBLANES = 8
HEAD_PAIRS = A_HEADS // 2
VMEM_LIMIT = 56 * 1024 * 1024

COL_QA, COL_KA, COL_VA, COL_GA = 0, A_WIDTH, 2 * A_WIDTH, 3 * A_WIDTH
COL_XB = 4 * A_WIDTH
COL_GB = COL_XB + LRU_WIDTH
COL_QC = COL_GB + LRU_WIDTH
COL_FC = COL_QC + C_KWIDTH
COL_IC = COL_FC + C_KWIDTH
COL_GC = COL_IC + C_VWIDTH

ATT_TQ = 512
ATT_TK = 2048
HGRN_CHUNK = 64
HGRN_GROUP = 256
HGRN_TILE = 512
HGRN_SAMPLE_ROWS = 8
ATT_SAMPLE_ROWS = 2
LRU_TILE = 512
PROJ_TM = 512


def _silu(x):
    return x * jax.nn.sigmoid(x)


def _params(sem, **kw):
    return pltpu.CompilerParams(dimension_semantics=sem, vmem_limit_bytes=VMEM_LIMIT, **kw)


def _inproj_kernel(x_ref, g_ref, w_ref, z_ref, *, n_chunk):
    x = x_ref[...]
    h = (x * lax.rsqrt(jnp.mean(x * x, axis=-1, keepdims=True) + RMS_EPS) * g_ref[...]).astype(BF16)
    for n in range(IN_WIDTH // n_chunk):
        sl = slice(n * n_chunk, (n + 1) * n_chunk)
        z_ref[:, sl] = jnp.dot(h, w_ref[:, sl], preferred_element_type=F32)


def _inproj(x2d, norm_g3, w_in_bf, layer):
    m = x2d.shape[0]
    tm = min(PROJ_TM, m)
    assert m % tm == 0
    return pl.pallas_call(
        functools.partial(_inproj_kernel, n_chunk=512),
        grid=(m // tm,),
        in_specs=[
            pl.BlockSpec((tm, D_MODEL), lambda i: (i, 0)),
            pl.BlockSpec((None, 1, D_MODEL), lambda i: (layer, 0, 0)),
            pl.BlockSpec((None, D_MODEL, IN_WIDTH), lambda i: (layer, 0, 0)),
        ],
        out_specs=pl.BlockSpec((tm, IN_WIDTH), lambda i: (i, 0)),
        out_shape=jax.ShapeDtypeStruct((m, IN_WIDTH), F32),
        compiler_params=_params(("parallel",)),
        name="inproj",
    )(x2d, norm_g3, w_in_bf)


def _outproj_kernel(x_ref, ma_ref, mb_ref, mc_ref, w_ref, g_ref, y_ref, *, final_norm):
    y = x_ref[...]
    y = y + jnp.dot(ma_ref[...].astype(BF16), w_ref[0:A_WIDTH, :], preferred_element_type=F32)
    y = y + jnp.dot(mb_ref[...].astype(BF16), w_ref[A_WIDTH:A_WIDTH + LRU_WIDTH, :], preferred_element_type=F32)
    y = y + jnp.dot(mc_ref[...].astype(BF16), w_ref[A_WIDTH + LRU_WIDTH:, :], preferred_element_type=F32)
    if final_norm:
        y = y * lax.rsqrt(jnp.mean(y * y, axis=-1, keepdims=True) + RMS_EPS) * g_ref[...]
    y_ref[...] = y


def _outproj(x2d, mix_a, mix_b, mix_c, w_out_bf, final_g2, layer, final_norm):
    m = x2d.shape[0]
    tm = min(PROJ_TM, m)
    assert m % tm == 0
    mspec = lambda w: pl.BlockSpec((tm, w), lambda i: (i, 0))
    return pl.pallas_call(
        functools.partial(_outproj_kernel, final_norm=final_norm),
        grid=(m // tm,),
        in_specs=[
            mspec(D_MODEL), mspec(A_WIDTH), mspec(LRU_WIDTH), mspec(C_VWIDTH),
            pl.BlockSpec((None, MIX_WIDTH, D_MODEL), lambda i: (layer, 0, 0)),
            pl.BlockSpec((1, D_MODEL), lambda i: (0, 0)),
        ],
        out_specs=mspec(D_MODEL),
        out_shape=jax.ShapeDtypeStruct((m, D_MODEL), F32),
        compiler_params=_params(("parallel",)),
        name="outproj",
    )(x2d, mix_a, mix_b, mix_c, w_out_bf, final_g2)


def _branch_count(delta):
    delta = np.asarray(delta)
    cnt = np.zeros(delta.shape, np.int32)
    for window, dil in DILATED:
        cnt += ((delta >= 0) & (delta <= window) & (delta % dil == 0)).astype(np.int32)
    return cnt


def _log_count(cnt):
    with np.errstate(divide="ignore"):
        return np.where(cnt > 0, np.log(np.maximum(cnt, 1).astype(np.float64)), -np.inf).astype(np.float32)


ATT_FAR_OFF = (DILATED[1][0] + ATT_TQ - 1) // ATT_TQ + 1


def _prompt_bias():
    qi = np.arange(ATT_TQ)[None, :]
    ki = np.arange(ATT_TQ)[:, None]
    tabs = []
    for off in range(ATT_FAR_OFF + 1):
        b = _log_count(_branch_count(off * ATT_TQ + qi - ki)) * np.float32(LOG2_E)
        tabs.append(np.concatenate([b, b], axis=1))
    return jnp.asarray(np.stack(tabs))


def _attn_pair_kernel(qa_ref, qb_ref, k_ref, v_ref, ga_ref, gb_ref, bias_ref, kt_in_ref, vt_in_ref,
                      oa_ref, ob_ref, kt_ref, vt_ref, *, n_q):
    t = pl.program_id(2)
    lane = lax.broadcasted_iota(jnp.int32, (1, LANES), 1)
    head_a = lane < HEAD_DIM
    nt = (((1,), (1,)), ((), ()))
    tn = (((0,), (0,)), ((), ()))

    def stacked(q_ref):
        q = q_ref[...] * (HEAD_DIM ** -0.5 * LOG2_E)
        return jnp.concatenate([jnp.where(head_a, q, 0.0), jnp.where(head_a, 0.0, q)], axis=0).astype(BF16)

    def step(q2, i, j, carry):
        m, l, acc = carry
        k0, k1 = j * ATT_TK, min((j + 1) * ATT_TK, (i + 1) * ATT_TQ)
        kb = k_ref[k0:k1, :].astype(BF16)
        vb = v_ref[k0:k1, :].astype(BF16)
        bias = [bias_ref[min(i - c, ATT_FAR_OFF)] for c in range(k0 // ATT_TQ, k1 // ATT_TQ)]
        bias = bias[0] if len(bias) == 1 else jnp.concatenate(bias, axis=0)
        s = lax.dot_general(kb, q2, nt, preferred_element_type=F32) + bias
        m_new = jnp.maximum(m, jnp.max(s, axis=0, keepdims=True))
        alpha = jnp.exp2(m - m_new)
        p = jnp.exp2(s - m_new)
        l = alpha * l + jnp.sum(p, axis=0, keepdims=True)
        pv = lax.dot_general(vb, p.astype(BF16), tn, preferred_element_type=F32)
        return m_new, l, alpha * acc + pv

    def finish(carry, g_ref):
        _, l, acc = carry
        o2 = (acc / l).T
        o = jnp.where(head_a, o2[:ATT_TQ], o2[ATT_TQ:])
        return (o * _silu(g_ref[...])).astype(BF16)

    def run(tt):
        i_a, i_b = tt, n_q - 1 - tt
        n_a, n_b = (i_a * ATT_TQ) // ATT_TK + 1, (i_b * ATT_TQ) // ATT_TK + 1
        q2a, q2b = stacked(qa_ref), stacked(qb_ref)
        init = (jnp.full((1, 2 * ATT_TQ), -jnp.inf, F32), jnp.zeros((1, 2 * ATT_TQ), F32),
                jnp.zeros((LANES, 2 * ATT_TQ), F32))
        ca, cb = init, init
        for j in range(max(n_a, n_b)):
            if j < n_b:
                cb = step(q2b, i_b, j, cb)
            if j < n_a:
                ca = step(q2a, i_a, j, ca)
        return finish(ca, ga_ref), finish(cb, gb_ref)

    oa, ob = lax.switch(t, [functools.partial(run, tt) for tt in range(n_q // 2)])
    oa_ref[...] = oa
    ob_ref[...] = ob

    @pl.when(t == 0)
    def _():
        s_len = k_ref.shape[0]
        for src, dst in ((k_ref, kt_ref), (v_ref, vt_ref)):
            for c in range(s_len // LANES):
                blk = src[c * LANES:(c + 1) * LANES, :].T
                dst[:, :, c * LANES:(c + 1) * LANES] = blk.reshape(2, HEAD_DIM, LANES)


def _attn_prompt(z3, bias, kt_all, vt_all, layer):
    b, s, _ = z3.shape
    n_q = s // ATT_TQ
    assert ATT_TK % ATT_TQ == 0 and s % ATT_TK == 0 and s <= MAX_WINDOW and ATT_TQ % DILATED[2][1] == 0
    assert n_q % 2 == 0
    half = n_q // 2
    col = lambda base: (lambda bi, p, t: (bi, 0, base // LANES + p))
    fwd = lambda base: pl.BlockSpec((None, ATT_TQ, LANES), lambda bi, p, t: (bi, t, base // LANES + p))
    rev = lambda base: pl.BlockSpec((None, ATT_TQ, LANES), lambda bi, p, t: (bi, n_q - 1 - t, base // LANES + p))
    win = pl.BlockSpec((None, None, 2, HEAD_DIM, s), lambda bi, p, t: (layer, bi, p, 0, 0))
    half_out = pl.BlockSpec((None, ATT_TQ, LANES), lambda bi, p, t: (bi, t, p))
    o_lo, o_hi, kt_all, vt_all = pl.pallas_call(
        functools.partial(_attn_pair_kernel, n_q=n_q),
        grid=(b, HEAD_PAIRS, half),
        in_specs=[
            fwd(COL_QA), rev(COL_QA),
            pl.BlockSpec((None, s, LANES), col(COL_KA)),
            pl.BlockSpec((None, s, LANES), col(COL_VA)),
            fwd(COL_GA), rev(COL_GA),
            pl.BlockSpec((ATT_FAR_OFF + 1, ATT_TQ, 2 * ATT_TQ), lambda bi, p, t: (0, 0, 0)),
            pl.BlockSpec(memory_space=pl.ANY),
            pl.BlockSpec(memory_space=pl.ANY),
        ],
        out_specs=[half_out, half_out, win, win],
        out_shape=[jax.ShapeDtypeStruct((b, half * ATT_TQ, A_WIDTH), BF16),
                   jax.ShapeDtypeStruct((b, half * ATT_TQ, A_WIDTH), BF16),
                   jax.ShapeDtypeStruct(kt_all.shape, F32), jax.ShapeDtypeStruct(vt_all.shape, F32)],
        input_output_aliases={7: 2, 8: 3},
        compiler_params=_params(("parallel", "parallel", "arbitrary")),
        name="attn_prompt",
    )(z3, z3, z3, z3, z3, z3, bias, kt_all, vt_all)
    o_hi = o_hi.reshape(b, half, ATT_TQ, A_WIDTH)[:, ::-1].reshape(b, half * ATT_TQ, A_WIDTH)
    return jnp.concatenate([o_lo, o_hi], axis=1), kt_all, vt_all


def _sample_bias(t_len):
    qpos = MAX_WINDOW + np.arange(t_len)[:, None]
    bias_c = _log_count(_branch_count(qpos - np.arange(MAX_WINDOW)[None, :]))
    bias_n = _log_count(_branch_count(qpos - (MAX_WINDOW + np.arange(t_len)[None, :])))
    return jnp.asarray(bias_c), jnp.asarray(bias_n)


def _split_heads(x):
    return jnp.stack([x[:, h * HEAD_DIM:(h + 1) * HEAD_DIM] for h in range(A_HEADS)], axis=0)


def _attn_sample_row(q, k, v, g, kt, vt, bias_c, bias_n):
    q3 = _split_heads(q * (HEAD_DIM ** -0.5))
    kn3 = _split_heads(k)
    vn3 = _split_heads(v)
    bdot = lambda a, b, ca, cb: lax.dot_general(a, b, (((ca,), (cb,)), ((0,), (0,))), preferred_element_type=F32)
    s_c = bdot(q3, kt, 2, 1) + bias_c[None]
    s_n = bdot(q3, kn3, 2, 2) + bias_n[None]
    m = jnp.maximum(jnp.max(s_c, axis=-1, keepdims=True), jnp.max(s_n, axis=-1, keepdims=True))
    p_c = jnp.exp(s_c - m)
    p_n = jnp.exp(s_n - m)
    l = jnp.sum(p_c, axis=-1, keepdims=True) + jnp.sum(p_n, axis=-1, keepdims=True)
    acc = bdot(p_c, vt, 2, 2) + bdot(p_n, vn3, 2, 1)
    o3 = acc / l
    o = jnp.concatenate([o3[h] for h in range(A_HEADS)], axis=-1)
    return o * _silu(g)


def _neg_expm1(x):
    series = -x * (1.0 + x * (0.5 + x * (1.0 / 6.0 + x * (1.0 / 24.0 + x * (1.0 / 120.0)))))
    return jnp.where(x > -0.1, series, 1.0 - jnp.exp(x))


def _softplus(y):
    return jnp.maximum(y, 0.0) + jnp.log1p(jnp.exp(-jnp.abs(y)))


def _lru_gates(xc, wa_ref, ba_ref, wx_ref, bx_ref, lam_ref):
    xcb = xc.astype(BF16)
    r = jax.nn.sigmoid(jnp.dot(xcb, wa_ref[...], preferred_element_type=F32) + ba_ref[...])
    i = jax.nn.sigmoid(jnp.dot(xcb, wx_ref[...], preferred_element_type=F32) + bx_ref[...])
    log_a = -LRU_C * r * _softplus(-lam_ref[...])
    a = jnp.exp(log_a)
    u = jnp.sqrt(_neg_expm1(2.0 * log_a)) * (i * xc)
    return a, u


def _scan8(a, u, axis, row):
    for sh in (1, 2, 4):
        keep = row >= sh
        a_prev = jnp.where(keep, pltpu.roll(a, sh, axis), 1.0)
        u_prev = jnp.where(keep, pltpu.roll(u, sh, axis), 0.0)
        u = a * u_prev + u
        a = a * a_prev
    return a, u


def _lru_prompt_kernel(xb_ref, gb_ref, cw_ref, cb_ref, wa_ref, ba_ref, wx_ref, bx_ref, lam_ref,
                       o_ref, hlast_ref, xp_scr, a_scr, u_scr, hc_scr, *, tile):
    t = pl.program_id(1)

    @pl.when(t == 0)
    def _():
        xp_scr[0:SUBLANES, :] = jnp.zeros((SUBLANES, LRU_WIDTH), F32)
        hc_scr[...] = jnp.zeros_like(hc_scr)

    xp_scr[SUBLANES:SUBLANES + tile, :] = xb_ref[...]
    cw = cw_ref[...]
    xc = cb_ref[...]
    for j in range(CONV_W):
        off = SUBLANES - (CONV_W - 1) + j
        xc = xc + xp_scr[off:off + tile, :] * cw[j:j + 1, :]
    xp_scr[0:SUBLANES, :] = xp_scr[tile:tile + SUBLANES, :]

    a, u = _lru_gates(xc, wa_ref, ba_ref, wx_ref, bx_ref, lam_ref)
    row = lax.broadcasted_iota(jnp.int32, (1, SUBLANES, 1), 1)
    grouped = (tile // SUBLANES, SUBLANES, LRU_WIDTH)
    a, u = _scan8(a.reshape(grouped), u.reshape(grouped), 1, row)
    a_scr[...] = a.reshape(tile, LRU_WIDTH)
    u_scr[...] = u.reshape(tile, LRU_WIDTH)

    def body(gi, hc):
        r0 = pl.multiple_of(gi * SUBLANES, SUBLANES)
        h = a_scr[pl.ds(r0, SUBLANES), :] * hc + u_scr[pl.ds(r0, SUBLANES), :]
        u_scr[pl.ds(r0, SUBLANES), :] = h
        return jnp.broadcast_to(h[SUBLANES - 1:SUBLANES, :], (SUBLANES, LRU_WIDTH))

    hc = lax.fori_loop(0, tile // SUBLANES, body, hc_scr[...])
    hc_scr[...] = hc
    o_ref[...] = (u_scr[...] * _silu(gb_ref[...])).astype(o_ref.dtype)

    @pl.when(t == pl.num_programs(1) - 1)
    def _():
        hlast_ref[...] = hc[0:1, :]


def _lru_weight_specs(layer, nargs):
    def spec(shape):
        idx = (layer,) + (0,) * len(shape)
        if nargs == 1:
            return pl.BlockSpec((None,) + shape, lambda a: idx)
        return pl.BlockSpec((None,) + shape, lambda a, b: idx)
    return [spec((CONV_W, LRU_WIDTH)), spec((1, LRU_WIDTH)),
            spec((LRU_WIDTH, LRU_WIDTH)), spec((1, LRU_WIDTH)),
            spec((LRU_WIDTH, LRU_WIDTH)), spec((1, LRU_WIDTH)), spec((1, LRU_WIDTH))]


def _lru_prompt(z3, lru_w, layer):
    b, s, _ = z3.shape
    tile = min(LRU_TILE, s)
    assert s % tile == 0
    return pl.pallas_call(
        functools.partial(_lru_prompt_kernel, tile=tile),
        grid=(b, s // tile),
        in_specs=[pl.BlockSpec((None, tile, LRU_WIDTH), lambda bi, t: (bi, t, COL_XB // LRU_WIDTH)),
                  pl.BlockSpec((None, tile, LRU_WIDTH), lambda bi, t: (bi, t, COL_GB // LRU_WIDTH))]
        + _lru_weight_specs(layer, 2),
        out_specs=[pl.BlockSpec((None, tile, LRU_WIDTH), lambda bi, t: (bi, t, 0)),
                   pl.BlockSpec((None, 1, LRU_WIDTH), lambda bi, t: (bi, 0, 0))],
        out_shape=[jax.ShapeDtypeStruct((b, s, LRU_WIDTH), BF16),
                   jax.ShapeDtypeStruct((b, 1, LRU_WIDTH), F32)],
        scratch_shapes=[pltpu.VMEM((tile + 2 * SUBLANES, LRU_WIDTH), F32),
                        pltpu.VMEM((tile, LRU_WIDTH), F32),
                        pltpu.VMEM((tile, LRU_WIDTH), F32),
                        pltpu.VMEM((SUBLANES, LRU_WIDTH), F32)],
        compiler_params=_params(("parallel", "arbitrary")),
        name="lru_prompt",
    )(z3, z3, *lru_w)


def _lru_sample_kernel(xb_ref, gb_ref, cs_ref, h0_ref, cw_ref, cb_ref, wa_ref, ba_ref, wx_ref, bx_ref, lam_ref,
                       o_ref, hlast_ref):
    x = xb_ref[...]
    nb, t_len, _ = x.shape
    cs = cs_ref[...]
    row = lax.broadcasted_iota(jnp.int32, (1, t_len, 1), 1)
    cw = cw_ref[...]
    xc = cb_ref[...] + x * cw[CONV_W - 1:CONV_W, :]
    for back in range(1, CONV_W):
        xs = pltpu.roll(x, back, 1)
        for r in range(back):
            src = CONV_W - 1 - back + r
            xs = jnp.where(row == r, cs[:, src:src + 1, :], xs)
        xc = xc + xs * cw[CONV_W - 1 - back:CONV_W - back, :]
    a, u = _lru_gates(xc.reshape(nb * t_len, LRU_WIDTH), wa_ref, ba_ref, wx_ref, bx_ref, lam_ref)
    a = a.reshape(nb, t_len, LRU_WIDTH)
    u = u.reshape(nb, t_len, LRU_WIDTH)
    a, u = _scan8(a, u, 1, row)
    h = a * h0_ref[...] + u
    o_ref[...] = h * _silu(gb_ref[...])
    hlast_ref[...] = h[:, t_len - 1:t_len, :]


def _lru_sample(zs3, state_conv, state_lru4, lru_w, layer):
    b, t_len, _ = zs3.shape
    assert t_len == SUBLANES
    return pl.pallas_call(
        _lru_sample_kernel,
        grid=(1,),
        in_specs=[pl.BlockSpec((b, t_len, LRU_WIDTH), lambda i: (0, 0, COL_XB // LRU_WIDTH)),
                  pl.BlockSpec((b, t_len, LRU_WIDTH), lambda i: (0, 0, COL_GB // LRU_WIDTH)),
                  pl.BlockSpec((None, b, CONV_W - 1, LRU_WIDTH), lambda i: (layer, 0, 0, 0)),
                  pl.BlockSpec((None, b, 1, LRU_WIDTH), lambda i: (layer, 0, 0, 0))]
        + _lru_weight_specs(layer, 1),
        out_specs=[pl.BlockSpec((b, t_len, LRU_WIDTH), lambda i: (0, 0, 0)),
                   pl.BlockSpec((b, 1, LRU_WIDTH), lambda i: (0, 0, 0))],
        out_shape=[jax.ShapeDtypeStruct((b, t_len, LRU_WIDTH), F32),
                   jax.ShapeDtypeStruct((b, 1, LRU_WIDTH), F32)],
        compiler_params=_params(("arbitrary",)),
        name="lru_sample",
    )(zs3, zs3, state_conv, state_lru4, *lru_w)


def _split3(x):
    hi = x.astype(BF16)
    r = x - hi.astype(F32)
    mid = r.astype(BF16)
    lo = (r - mid.astype(F32)).astype(BF16)
    return hi, mid, lo


def _pad_rows(x, n):
    if x.shape[0] == n:
        return x
    return jnp.concatenate([x, jnp.zeros((n - x.shape[0], x.shape[1]), x.dtype)], axis=0)


def _lower_bound(logits, layer):
    e = jnp.exp(logits - jnp.max(logits, axis=0, keepdims=True))
    den = jnp.sum(e, axis=0, keepdims=True)
    num = jnp.zeros_like(den)
    for r in range(1, layer + 1):
        num = num + e[r:r + 1, :]
    return num / den


def _chunk_cumsum(x, chunk):
    n = x.shape[0]
    if chunk == SUBLANES:
        x3 = x.reshape(n // SUBLANES, SUBLANES, x.shape[1])
        sub = lax.broadcasted_iota(jnp.int32, (1, SUBLANES, 1), 1)
        for sh in (1, 2, 4):
            x3 = x3 + jnp.where(sub >= sh, pltpu.roll(x3, sh, 1), 0.0)
        return x3.reshape(n, x.shape[1])
    row = lax.broadcasted_iota(jnp.int32, (n, 1), 0)
    col = lax.broadcasted_iota(jnp.int32, (1, n), 1)
    tri = ((row >= col) & (row // chunk == col // chunk)).astype(BF16)
    out = None
    for part in _split3(x):
        term = jnp.dot(tri, part, preferred_element_type=F32)
        out = term if out is None else out + term
    return out


def _hgrn_tile(qc2, fc2, v, gc, st, lb2, cg, chunk, carry=True):
    n = v.shape[0]
    nc = n // chunk
    nblk = chunk // SUBLANES
    lane = lax.broadcasted_iota(jnp.int32, (1, LANES), 1)
    head_a = lane < C_VDIM
    sub = lax.broadcasted_iota(jnp.int32, (1, SUBLANES, 1), 1)
    blocked = lambda x: x.reshape(n // SUBLANES, SUBLANES, LANES)
    chunked = lambda x: x.reshape(nc, chunk, LANES)
    bnt = (((2,), (2,)), ((0,), (0,)))
    bnn = (((2,), (1,)), ((0,), (0,)))
    nt = (((1,), (1,)), ((), ()))

    q2 = _silu(qc2) * (C_KDIM ** -0.5)
    g2 = lb2 + (1.0 - lb2) * jax.nn.sigmoid(fc2)
    k2 = 1.0 - g2
    b2 = _chunk_cumsum(jnp.log(g2) * LOG2_E, chunk)
    v3 = blocked(v)
    vc = chunked(v).astype(BF16)
    local, q_states, decays, k_ends = [], [], [], []
    for a in range(2):
        hs = slice(a * C_KDIM, (a + 1) * C_KDIM)
        q, k, b = q2[:, hs], k2[:, hs], b2[:, hs]
        q3, k3, g3 = blocked(q), blocked(k), blocked(g2[:, hs])
        o3 = jnp.sum(q3 * k3, axis=-1, keepdims=True) * v3
        decay = g3
        for d in range(1, SUBLANES):
            g_back = pltpu.roll(g3, d, 1)
            w = q3 * (1.0 - g_back) * decay
            a_d = jnp.where(sub >= d, jnp.sum(w, axis=-1, keepdims=True), 0.0)
            o3 = o3 + a_d * pltpu.roll(v3, d, 1)
            decay = decay * g_back
        o = o3.reshape(n, LANES)
        qc_, kc_, bc_ = chunked(q), chunked(k), chunked(b)
        if nblk > 1:
            blocks = [jnp.zeros((nc, SUBLANES, chunk), F32)]
            for i in range(1, nblk):
                lo = i * SUBLANES
                ref = bc_[:, lo - 1:lo, :]
                kt = kc_[:, :lo, :] * jnp.exp2(ref - bc_[:, :lo, :])
                kt = jnp.concatenate([kt, jnp.zeros((nc, chunk - lo, LANES), F32)], axis=1).astype(BF16)
                qt = (qc_[:, lo:lo + SUBLANES, :] * jnp.exp2(bc_[:, lo:lo + SUBLANES, :] - ref)).astype(BF16)
                blocks.append(lax.dot_general(qt, kt, bnt, preferred_element_type=F32))
            attn = jnp.concatenate(blocks, axis=1).astype(BF16)
            o = o + lax.dot_general(attn, vc, bnn, preferred_element_type=F32).reshape(n, LANES)
        local.append(o)
        last = bc_[:, chunk - 1:chunk, :]
        decays.append(jnp.exp2(last))
        k_ends.append(kc_ * jnp.exp2(last - bc_))
        q_states.append((qc_ * jnp.exp2(bc_)).astype(BF16))
    top = lax.broadcasted_iota(jnp.int32, (LANES, 1), 0) < C_VDIM
    pad = lambda x: _pad_rows(x, LANES).astype(BF16)
    inter = [[], []]
    new_states = []
    for c in range(nc):
        st_c = st if carry else st[c]
        st_b = st_c.astype(BF16)
        for a in range(2):
            inter[a].append(lax.dot_general(q_states[a][c], st_b, nt, preferred_element_type=F32))
        v_t = _pad_rows(v[c * chunk:(c + 1) * chunk], LANES).T.astype(BF16)
        upd_a = jnp.dot(v_t, pad(k_ends[0][c]), preferred_element_type=F32)
        upd_b = jnp.dot(v_t, pad(k_ends[1][c]), preferred_element_type=F32)
        st_c = jnp.where(top, st_c * decays[0][c] + upd_a, st_c * decays[1][c] + upd_b)
        if carry:
            st = st_c
        else:
            new_states.append(st_c)
    if not carry:
        st = jnp.stack(new_states)
    cat = lambda xs: xs[0] if len(xs) == 1 else jnp.concatenate(xs, axis=0)
    o = jnp.where(head_a, local[0] + cat(inter[0]), local[1] + cat(inter[1]))
    sq = o * o
    ms_a = jnp.sum(jnp.where(head_a, sq, 0.0), axis=-1, keepdims=True) * (1.0 / C_VDIM)
    ms_b = jnp.sum(jnp.where(head_a, 0.0, sq), axis=-1, keepdims=True) * (1.0 / C_VDIM)
    inv = jnp.where(head_a, lax.rsqrt(ms_a + RMS_EPS), lax.rsqrt(ms_b + RMS_EPS))
    return o * inv * cg * _silu(gc), st


def _hgrn_prompt_kernel(qc_ref, fc_ref, ic_ref, gc_ref, lbl_ref, cg_ref,
                        sq_ref, sk_ref, sv_ref, sg_ref, kt_ref, vt_ref, bias_c_ref, bias_n_ref,
                        o_ref, st_ref, so_ref, st_scr, *, layer, tile, row_steps):
    t = pl.program_id(2)
    step = (pl.program_id(0) * pl.num_programs(1) + pl.program_id(1)) * pl.num_programs(2) + t

    @pl.when(step < row_steps)
    def _():
        for r in range(sq_ref.shape[0]):
            so_ref[r] = _attn_sample_row(sq_ref[r], sk_ref[r], sv_ref[r], sg_ref[r], kt_ref[r], vt_ref[r],
                                         bias_c_ref[...], bias_n_ref[...])

    @pl.when(t == 0)
    def _():
        st_scr[...] = jnp.zeros_like(st_scr)

    lb2 = _lower_bound(lbl_ref[...], layer)
    cg = cg_ref[...]

    def body(c, carry):
        r0 = pl.multiple_of(c * HGRN_GROUP, HGRN_GROUP)
        rows = pl.ds(r0, HGRN_GROUP)
        out, st_new = _hgrn_tile(qc_ref[rows, :], fc_ref[rows, :], ic_ref[rows, :], gc_ref[rows, :],
                                 st_scr[...], lb2, cg, HGRN_CHUNK)
        o_ref[rows, :] = out.astype(o_ref.dtype)
        st_scr[...] = st_new
        return carry

    lax.fori_loop(0, tile // HGRN_GROUP, body, 0)

    @pl.when(t == pl.num_programs(2) - 1)
    def _():
        st_ref[...] = st_scr[...]


def _hgrn_prompt(z3, lb_logits, cnorm3, zs3, cache_kt, cache_vt, bias_s, layer):
    b, s, _ = z3.shape
    bs, t_len, _ = zs3.shape
    w_buf = cache_kt.shape[-1]
    tile = min(HGRN_TILE, s)
    assert s % tile == 0 and tile % HGRN_CHUNK == 0
    kw, vw = 2 * C_KDIM, 2 * C_VDIM
    n_t = s // tile
    rows = min(ATT_SAMPLE_ROWS, bs)
    row_steps = bs // rows
    assert bs % rows == 0 and row_steps <= b * HEAD_PAIRS * n_t
    grp = lambda bi, p, t: jnp.minimum((bi * HEAD_PAIRS + p) * n_t + t, row_steps - 1)
    zcol = lambda base: pl.BlockSpec((rows, t_len, A_WIDTH), lambda bi, p, t: (grp(bi, p, t), 0, base // A_WIDTH))
    cache = pl.BlockSpec((None, rows, A_HEADS, HEAD_DIM, w_buf), lambda bi, p, t: (layer, grp(bi, p, t), 0, 0, 0))
    bias_c, bias_n = bias_s
    return pl.pallas_call(
        functools.partial(_hgrn_prompt_kernel, layer=layer, tile=tile, row_steps=row_steps),
        grid=(b, HEAD_PAIRS, n_t),
        in_specs=[pl.BlockSpec((None, tile, kw), lambda bi, p, t: (bi, t, COL_QC // kw + p)),
                  pl.BlockSpec((None, tile, kw), lambda bi, p, t: (bi, t, COL_FC // kw + p)),
                  pl.BlockSpec((None, tile, vw), lambda bi, p, t: (bi, t, COL_IC // vw + p)),
                  pl.BlockSpec((None, tile, vw), lambda bi, p, t: (bi, t, COL_GC // vw + p)),
                  pl.BlockSpec((DEPTH, kw), lambda bi, p, t: (0, p)),
                  pl.BlockSpec((None, 1, vw), lambda bi, p, t: (layer, 0, p)),
                  zcol(COL_QA), zcol(COL_KA), zcol(COL_VA), zcol(COL_GA), cache, cache,
                  pl.BlockSpec(bias_c.shape, lambda bi, p, t: (0, 0)),
                  pl.BlockSpec(bias_n.shape, lambda bi, p, t: (0, 0))],
        out_specs=[pl.BlockSpec((None, tile, vw), lambda bi, p, t: (bi, t, p)),
                   pl.BlockSpec((None, None, LANES, LANES), lambda bi, p, t: (bi, p, 0, 0)),
                   pl.BlockSpec((rows, t_len, A_WIDTH), lambda bi, p, t: (grp(bi, p, t), 0, 0))],
        out_shape=[jax.ShapeDtypeStruct((b, s, C_VWIDTH), BF16),
                   jax.ShapeDtypeStruct((b, HEAD_PAIRS, LANES, LANES), F32),
                   jax.ShapeDtypeStruct((bs, t_len, A_WIDTH), F32)],
        scratch_shapes=[pltpu.VMEM((LANES, LANES), F32)],
        compiler_params=_params(("arbitrary", "arbitrary", "arbitrary")),
        name="hgrn_prompt",
    )(z3, z3, z3, z3, lb_logits, cnorm3, zs3, zs3, zs3, zs3, cache_kt, cache_vt, bias_c, bias_n)


def _hgrn_sample_kernel(qc_ref, fc_ref, ic_ref, gc_ref, st_in_ref, lbl_ref, cg_ref, o_ref, st_ref, *, layer):
    lb = _lower_bound(lbl_ref[...], layer)
    rows, t_len, _ = qc_ref.shape
    kw, vw = 2 * C_KDIM, 2 * C_VDIM
    flat = lambda ref, sl: ref[:, :, sl].reshape(rows * t_len, sl.stop - sl.start)
    for p in range(HEAD_PAIRS):
        ks, vs = slice(p * kw, (p + 1) * kw), slice(p * vw, (p + 1) * vw)
        out, st_new = _hgrn_tile(flat(qc_ref, ks), flat(fc_ref, ks), flat(ic_ref, vs), flat(gc_ref, vs),
                                 st_in_ref[:, p], lb[:, ks], cg_ref[:, vs], t_len, carry=False)
        o_ref[:, :, vs] = out.reshape(rows, t_len, vw)
        st_ref[:, p] = st_new


def _hgrn_sample(zs3, state_t, lb_logits, cnorm3, layer):
    b, t_len, _ = zs3.shape
    rows = min(HGRN_SAMPLE_ROWS, b)
    assert t_len == SUBLANES and b % rows == 0
    return pl.pallas_call(
        functools.partial(_hgrn_sample_kernel, layer=layer),
        grid=(b // rows,),
        in_specs=[pl.BlockSpec((rows, t_len, C_KWIDTH), lambda bi: (bi, 0, COL_QC // C_KWIDTH)),
                  pl.BlockSpec((rows, t_len, C_KWIDTH), lambda bi: (bi, 0, COL_FC // C_KWIDTH)),
                  pl.BlockSpec((rows, t_len, C_VWIDTH), lambda bi: (bi, 0, COL_IC // C_VWIDTH)),
                  pl.BlockSpec((rows, t_len, C_VWIDTH), lambda bi: (bi, 0, COL_GC // C_VWIDTH)),
                  pl.BlockSpec((None, rows, HEAD_PAIRS, LANES, LANES), lambda bi: (layer, bi, 0, 0, 0)),
                  pl.BlockSpec((DEPTH, C_KWIDTH), lambda bi: (0, 0)),
                  pl.BlockSpec((None, 1, C_VWIDTH), lambda bi: (layer, 0, 0))],
        out_specs=[pl.BlockSpec((rows, t_len, C_VWIDTH), lambda bi: (bi, 0, 0)),
                   pl.BlockSpec((rows, HEAD_PAIRS, LANES, LANES), lambda bi: (bi, 0, 0, 0))],
        out_shape=[jax.ShapeDtypeStruct((b, t_len, C_VWIDTH), F32),
                   jax.ShapeDtypeStruct((b, HEAD_PAIRS, LANES, LANES), F32)],
        compiler_params=_params(("parallel",)),
        name="hgrn_sample",
    )(zs3, zs3, zs3, zs3, state_t, lb_logits, cnorm3)


def _block_diag(w):
    depth, nb, d, _ = w.shape
    eye = jnp.eye(nb, dtype=w.dtype)
    full = jnp.einsum("lhij,hg->lhigj", w, eye)
    return full.reshape(depth, nb * d, nb * d).astype(BF16)


def _state_to_pairs(st):
    lead = st.shape[:-3]
    t = jnp.swapaxes(st, -1, -2)
    return t.reshape(lead + (HEAD_PAIRS, 2 * C_VDIM, C_KDIM))


def _pairs_to_state(sp):
    lead = sp.shape[:-3]
    t = sp.reshape(lead + (C_HEADS, C_VDIM, C_KDIM))
    return jnp.swapaxes(t, -1, -2)


def kernel(x_prompt, x_sample, cache_k_win, cache_v_win, state_conv, state_lru, state_hgrn, norm_g, w_in, conv_w,
           conv_b, lru_w_a, lru_b_a, lru_w_x, lru_b_x, lru_lambda, hgrn_lb_logits, hgrn_norm_g, w_out, final_norm_g):
    bp, sp, _ = x_prompt.shape
    bs, ts, _ = x_sample.shape
    w_buf = cache_k_win.shape[2]
    assert w_buf == MAX_WINDOW and sp <= MAX_WINDOW

    w_in_bf = w_in.astype(BF16)
    w_out_bf = w_out.astype(BF16)
    norm_g3 = norm_g.reshape(DEPTH, 1, D_MODEL)
    final_g2 = final_norm_g.reshape(1, D_MODEL)
    row3 = lambda p: p.reshape(DEPTH, 1, -1)
    lru_w = (conv_w, row3(conv_b), _block_diag(lru_w_a), row3(lru_b_a), _block_diag(lru_w_x), row3(lru_b_x),
             row3(lru_lambda))
    cnorm3 = row3(hgrn_norm_g)
    cache_kt = jnp.transpose(cache_k_win, (0, 1, 3, 4, 2))
    cache_vt = jnp.transpose(cache_v_win, (0, 1, 3, 4, 2))
    state_lru4 = state_lru.reshape(DEPTH, bs, 1, LRU_WIDTH)
    state_t = _state_to_pairs(state_hgrn)
    bias_p = _prompt_bias()
    bias_s = _sample_bias(ts)

    xp = x_prompt.reshape(bp * sp, D_MODEL)
    xs = x_sample.reshape(bs * ts, D_MODEL)
    outs = {k: [] for k in ("ks", "vs", "cp", "cs", "lp", "ls", "hp", "hs")}
    kt_all = jnp.zeros((DEPTH, bp, A_HEADS, HEAD_DIM, sp), F32)
    vt_all = jnp.zeros((DEPTH, bp, A_HEADS, HEAD_DIM, sp), F32)
    for layer in range(DEPTH):
        last = layer == DEPTH - 1
        zs = _inproj(xs, norm_g3, w_in_bf, layer)
        zs3 = zs.reshape(bs, ts, IN_WIDTH)
        z = _inproj(xp, norm_g3, w_in_bf, layer)
        z3 = z.reshape(bp, sp, IN_WIDTH)
        mix_a, kt_all, vt_all = _attn_prompt(z3, bias_p, kt_all, vt_all, layer)
        mix_b, h_last = _lru_prompt(z3, lru_w, layer)
        mix_c, st_p, smix_a = _hgrn_prompt(z3, hgrn_lb_logits, cnorm3, zs3, cache_kt, cache_vt, bias_s, layer)
        xp = _outproj(xp, mix_a.reshape(bp * sp, -1), mix_b.reshape(bp * sp, -1), mix_c.reshape(bp * sp, -1),
                      w_out_bf, final_g2, layer, last)
        outs["cp"].append(z3[:, sp - (CONV_W - 1):, COL_XB:COL_XB + LRU_WIDTH])
        outs["lp"].append(h_last.reshape(bp, LRU_WIDTH))
        outs["hp"].append(_pairs_to_state(st_p))
        smix_b, sh_last = _lru_sample(zs3, state_conv, state_lru4, lru_w, layer)
        smix_c, st_s = _hgrn_sample(zs3, state_t, hgrn_lb_logits, cnorm3, layer)
        xs = _outproj(xs, smix_a.reshape(bs * ts, -1), smix_b.reshape(bs * ts, -1), smix_c.reshape(bs * ts, -1),
                      w_out_bf, final_g2, layer, last)
        outs["ks"].append(zs3[:, :, COL_KA:COL_KA + A_WIDTH].reshape(bs, ts, A_HEADS, HEAD_DIM))
        outs["vs"].append(zs3[:, :, COL_VA:COL_VA + A_WIDTH].reshape(bs, ts, A_HEADS, HEAD_DIM))
        outs["cs"].append(zs3[:, ts - (CONV_W - 1):, COL_XB:COL_XB + LRU_WIDTH])
        outs["ls"].append(sh_last.reshape(bs, LRU_WIDTH))
        outs["hs"].append(_pairs_to_state(st_s))
    st = lambda k: jnp.stack(outs[k])
    k_win_p = jnp.transpose(kt_all, (0, 1, 4, 2, 3))
    v_win_p = jnp.transpose(vt_all, (0, 1, 4, 2, 3))
    return (xp.reshape(bp, sp, D_MODEL), xs.reshape(bs, ts, D_MODEL), k_win_p, v_win_p, st("ks"), st("vs"),
            st("cp"), st("cs"), st("lp"), st("ls"), st("hp"), st("hs"))
```

```python
import functools

import jax
import jax.numpy as jnp
import numpy as np
from jax import lax
from jax.experimental import pallas as pl
from jax.experimental.pallas import tpu as pltpu

F32 = jnp.float32
BF16 = jnp.bfloat16

D_MODEL = 1024
DEPTH = 4
HEAD_DIM = 64
A_HEADS = 6
A_WIDTH = A_HEADS * HEAD_DIM
DILATED = ((128, 1), (512, 4), (2048, 16))
MAX_WINDOW = 2048
LRU_BLOCKS = 6
LRU_BLOCK_DIM = 64
LRU_WIDTH = LRU_BLOCKS * LRU_BLOCK_DIM
CONV_W = 4
LRU_C = 8.0
C_HEADS = 6
C_KDIM = 128
C_VDIM = 64
C_KWIDTH = C_HEADS * C_KDIM
C_VWIDTH = C_HEADS * C_VDIM
MIX_WIDTH = A_WIDTH + LRU_WIDTH + C_VWIDTH
IN_WIDTH = 4 * A_WIDTH + 2 * LRU_WIDTH + 2 * C_KWIDTH + 2 * C_VWIDTH
RMS_EPS = 1e-6
LOG2_E = 1.4426950408889634

LANES = 128
SUBLANES = 8
HEAD_PAIRS = A_HEADS // 2
VMEM_LIMIT = 56 * 1024 * 1024

COL_QA, COL_KA, COL_VA, COL_GA = 0, A_WIDTH, 2 * A_WIDTH, 3 * A_WIDTH
COL_XB = 4 * A_WIDTH
COL_GB = COL_XB + LRU_WIDTH
COL_QC = COL_GB + LRU_WIDTH
COL_FC = COL_QC + C_KWIDTH
COL_IC = COL_FC + C_KWIDTH
COL_GC = COL_IC + C_VWIDTH

ATT_TQ = 512
HGRN_CHUNK = 64
HGRN_GROUP = 256
HGRN_TILE = 512
HGRN_SAMPLE_ROWS = 8
ATT_SAMPLE_ROWS = 2
LRU_TILE = 512
PROJ_TM = 512


def _silu(x):
    return x * jax.nn.sigmoid(x)


def _params(sem, **kw):
    return pltpu.CompilerParams(dimension_semantics=sem, vmem_limit_bytes=VMEM_LIMIT, **kw)


def _inproj_kernel(x_ref, g_ref, w_ref, z_ref, *, n_chunk):
    x = x_ref[...]
    h = (x * lax.rsqrt(jnp.mean(x * x, axis=-1, keepdims=True) + RMS_EPS) * g_ref[...]).astype(BF16)
    for n in range(IN_WIDTH // n_chunk):
        sl = slice(n * n_chunk, (n + 1) * n_chunk)
        z_ref[:, sl] = jnp.dot(h, w_ref[:, sl], preferred_element_type=F32)


def _inproj(x2d, norm_g3, w_in_bf, layer):
    m = x2d.shape[0]
    tm = min(PROJ_TM, m)
    assert m % tm == 0
    return pl.pallas_call(
        functools.partial(_inproj_kernel, n_chunk=512),
        grid=(m // tm,),
        in_specs=[
            pl.BlockSpec((tm, D_MODEL), lambda i: (i, 0)),
            pl.BlockSpec((None, 1, D_MODEL), lambda i: (layer, 0, 0)),
            pl.BlockSpec((None, D_MODEL, IN_WIDTH), lambda i: (layer, 0, 0)),
        ],
        out_specs=pl.BlockSpec((tm, IN_WIDTH), lambda i: (i, 0)),
        out_shape=jax.ShapeDtypeStruct((m, IN_WIDTH), F32),
        compiler_params=_params(("parallel",)),
        name="inproj",
    )(x2d, norm_g3, w_in_bf)


def _outproj_kernel(x_ref, ma_ref, mb_ref, mc_ref, w_ref, g_ref, y_ref, *, final_norm):
    y = x_ref[...]
    y = y + jnp.dot(ma_ref[...].astype(BF16), w_ref[0:A_WIDTH, :], preferred_element_type=F32)
    y = y + jnp.dot(mb_ref[...].astype(BF16), w_ref[A_WIDTH:A_WIDTH + LRU_WIDTH, :], preferred_element_type=F32)
    y = y + jnp.dot(mc_ref[...].astype(BF16), w_ref[A_WIDTH + LRU_WIDTH:, :], preferred_element_type=F32)
    if final_norm:
        y = y * lax.rsqrt(jnp.mean(y * y, axis=-1, keepdims=True) + RMS_EPS) * g_ref[...]
    y_ref[...] = y


def _outproj(x2d, mix_a, mix_b, mix_c, w_out_bf, final_g2, layer, final_norm):
    m = x2d.shape[0]
    tm = min(PROJ_TM, m)
    assert m % tm == 0
    mspec = lambda w: pl.BlockSpec((tm, w), lambda i: (i, 0))
    return pl.pallas_call(
        functools.partial(_outproj_kernel, final_norm=final_norm),
        grid=(m // tm,),
        in_specs=[
            mspec(D_MODEL), mspec(A_WIDTH), mspec(LRU_WIDTH), mspec(C_VWIDTH),
            pl.BlockSpec((None, MIX_WIDTH, D_MODEL), lambda i: (layer, 0, 0)),
            pl.BlockSpec((1, D_MODEL), lambda i: (0, 0)),
        ],
        out_specs=mspec(D_MODEL),
        out_shape=jax.ShapeDtypeStruct((m, D_MODEL), F32),
        compiler_params=_params(("parallel",)),
        name="outproj",
    )(x2d, mix_a, mix_b, mix_c, w_out_bf, final_g2)


def _branch_count(delta):
    delta = np.asarray(delta)
    cnt = np.zeros(delta.shape, np.int32)
    for window, dil in DILATED:
        cnt += ((delta >= 0) & (delta <= window) & (delta % dil == 0)).astype(np.int32)
    return cnt


def _log_count(cnt):
    with np.errstate(divide="ignore"):
        return np.where(cnt > 0, np.log(np.maximum(cnt, 1).astype(np.float64)), -np.inf).astype(np.float32)


ATT_FAR_OFF = (DILATED[1][0] + ATT_TQ - 1) // ATT_TQ + 1


def _prompt_bias():
    qi = np.arange(ATT_TQ)[None, :]
    ki = np.arange(ATT_TQ)[:, None]
    tabs = []
    for off in range(ATT_FAR_OFF + 1):
        b = _log_count(_branch_count(off * ATT_TQ + qi - ki)) * np.float32(LOG2_E)
        tabs.append(np.concatenate([b, b], axis=1))
    return jnp.asarray(np.stack(tabs))


def _attn_seq_kernel(q_ref, k_ref, v_ref, g_ref, bias_ref, kt_in_ref, vt_in_ref, o_ref, kt_ref, vt_ref):
    s_len = k_ref.shape[0]
    lane = lax.broadcasted_iota(jnp.int32, (1, LANES), 1)
    head_a = lane < HEAD_DIM
    nt = (((1,), (1,)), ((), ()))
    tn = (((0,), (0,)), ((), ()))
    for i in range(s_len // ATT_TQ):
        rows = slice(i * ATT_TQ, (i + 1) * ATT_TQ)
        q = q_ref[rows, :] * (HEAD_DIM ** -0.5 * LOG2_E)
        q2 = jnp.concatenate([jnp.where(head_a, q, 0.0), jnp.where(head_a, 0.0, q)], axis=0).astype(BF16)
        n_keys = (i + 1) * ATT_TQ
        kb = k_ref[0:n_keys, :].astype(BF16)
        vb = v_ref[0:n_keys, :].astype(BF16)
        bias = [bias_ref[min(i - c, ATT_FAR_OFF)] for c in range(i + 1)]
        bias = bias[0] if len(bias) == 1 else jnp.concatenate(bias, axis=0)
        s = lax.dot_general(kb, q2, nt, preferred_element_type=F32) + bias
        m = jnp.max(s, axis=0, keepdims=True)
        p = jnp.exp2(s - m)
        l = jnp.sum(p, axis=0, keepdims=True)
        acc = lax.dot_general(vb, p.astype(BF16), tn, preferred_element_type=F32)
        o2 = (acc / l).T
        o = jnp.where(head_a, o2[:ATT_TQ], o2[ATT_TQ:])
        o_ref[rows, :] = (o * _silu(g_ref[rows, :])).astype(o_ref.dtype)
    for src, dst in ((k_ref, kt_ref), (v_ref, vt_ref)):
        for c in range(s_len // LANES):
            blk = src[c * LANES:(c + 1) * LANES, :].T
            dst[:, :, c * LANES:(c + 1) * LANES] = blk.reshape(2, HEAD_DIM, LANES)


def _attn_prompt(z3, bias, kt_all, vt_all, layer):
    b, s, _ = z3.shape
    assert s % ATT_TQ == 0 and s <= MAX_WINDOW and ATT_TQ % DILATED[2][1] == 0
    col = lambda base: pl.BlockSpec((None, s, LANES), lambda bi, p: (bi, 0, base // LANES + p))
    win = pl.BlockSpec((None, None, 2, HEAD_DIM, s), lambda bi, p: (layer, bi, p, 0, 0))
    return pl.pallas_call(
        _attn_seq_kernel,
        grid=(b, HEAD_PAIRS),
        in_specs=[col(COL_QA), col(COL_KA), col(COL_VA), col(COL_GA),
                  pl.BlockSpec((ATT_FAR_OFF + 1, ATT_TQ, 2 * ATT_TQ), lambda bi, p: (0, 0, 0)),
                  pl.BlockSpec(memory_space=pl.ANY),
                  pl.BlockSpec(memory_space=pl.ANY)],
        out_specs=[pl.BlockSpec((None, s, LANES), lambda bi, p: (bi, 0, p)), win, win],
        out_shape=[jax.ShapeDtypeStruct((b, s, A_WIDTH), BF16),
                   jax.ShapeDtypeStruct(kt_all.shape, F32), jax.ShapeDtypeStruct(vt_all.shape, F32)],
        input_output_aliases={5: 1, 6: 2},
        compiler_params=_params(("parallel", "parallel")),
        name="attn_prompt",
    )(z3, z3, z3, z3, bias, kt_all, vt_all)


def _sample_bias(t_len):
    qpos = MAX_WINDOW + np.arange(t_len)[:, None]
    bias_c = _log_count(_branch_count(qpos - np.arange(MAX_WINDOW)[None, :]))
    bias_n = _log_count(_branch_count(qpos - (MAX_WINDOW + np.arange(t_len)[None, :])))
    return jnp.asarray(bias_c), jnp.asarray(bias_n)


def _split_heads(x):
    return jnp.stack([x[:, h * HEAD_DIM:(h + 1) * HEAD_DIM] for h in range(A_HEADS)], axis=0)


def _attn_sample_row(q, k, v, g, kt, vt, bias_c, bias_n):
    q3 = _split_heads(q * (HEAD_DIM ** -0.5))
    kn3 = _split_heads(k)
    vn3 = _split_heads(v)
    bdot = lambda a, b, ca, cb: lax.dot_general(a, b, (((ca,), (cb,)), ((0,), (0,))), preferred_element_type=F32)
    s_c = bdot(q3, kt, 2, 1) + bias_c[None]
    s_n = bdot(q3, kn3, 2, 2) + bias_n[None]
    m = jnp.maximum(jnp.max(s_c, axis=-1, keepdims=True), jnp.max(s_n, axis=-1, keepdims=True))
    p_c = jnp.exp(s_c - m)
    p_n = jnp.exp(s_n - m)
    l = jnp.sum(p_c, axis=-1, keepdims=True) + jnp.sum(p_n, axis=-1, keepdims=True)
    acc = bdot(p_c, vt, 2, 2) + bdot(p_n, vn3, 2, 1)
    o3 = acc / l
    o = jnp.concatenate([o3[h] for h in range(A_HEADS)], axis=-1)
    return o * _silu(g)


def _neg_expm1(x):
    series = -x * (1.0 + x * (0.5 + x * (1.0 / 6.0 + x * (1.0 / 24.0 + x * (1.0 / 120.0)))))
    return jnp.where(x > -0.1, series, 1.0 - jnp.exp(x))


def _softplus(y):
    return jnp.maximum(y, 0.0) + jnp.log1p(jnp.exp(-jnp.abs(y)))


def _lru_gates(xc, wa_ref, ba_ref, wx_ref, bx_ref, lam_ref):
    xcb = xc.astype(BF16)
    r = jax.nn.sigmoid(jnp.dot(xcb, wa_ref[...], preferred_element_type=F32) + ba_ref[...])
    i = jax.nn.sigmoid(jnp.dot(xcb, wx_ref[...], preferred_element_type=F32) + bx_ref[...])
    log_a = -LRU_C * r * _softplus(-lam_ref[...])
    a = jnp.exp(log_a)
    u = jnp.sqrt(_neg_expm1(2.0 * log_a)) * (i * xc)
    return a, u


def _scan8(a, u, axis, row):
    for sh in (1, 2, 4):
        keep = row >= sh
        a_prev = jnp.where(keep, pltpu.roll(a, sh, axis), 1.0)
        u_prev = jnp.where(keep, pltpu.roll(u, sh, axis), 0.0)
        u = a * u_prev + u
        a = a * a_prev
    return a, u


def _lru_prompt_kernel(xb_ref, gb_ref, cw_ref, cb_ref, wa_ref, ba_ref, wx_ref, bx_ref, lam_ref,
                       o_ref, hlast_ref, xp_scr, a_scr, u_scr, hc_scr, *, tile):
    t = pl.program_id(1)

    @pl.when(t == 0)
    def _():
        xp_scr[0:SUBLANES, :] = jnp.zeros((SUBLANES, LRU_WIDTH), F32)
        hc_scr[...] = jnp.zeros_like(hc_scr)

    xp_scr[SUBLANES:SUBLANES + tile, :] = xb_ref[...]
    cw = cw_ref[...]
    xc = cb_ref[...]
    for j in range(CONV_W):
        off = SUBLANES - (CONV_W - 1) + j
        xc = xc + xp_scr[off:off + tile, :] * cw[j:j + 1, :]
    xp_scr[0:SUBLANES, :] = xp_scr[tile:tile + SUBLANES, :]

    a, u = _lru_gates(xc, wa_ref, ba_ref, wx_ref, bx_ref, lam_ref)
    row = lax.broadcasted_iota(jnp.int32, (1, SUBLANES, 1), 1)
    grouped = (tile // SUBLANES, SUBLANES, LRU_WIDTH)
    a, u = _scan8(a.reshape(grouped), u.reshape(grouped), 1, row)
    a_scr[...] = a.reshape(tile, LRU_WIDTH)
    u_scr[...] = u.reshape(tile, LRU_WIDTH)

    def body(gi, hc):
        r0 = pl.multiple_of(gi * SUBLANES, SUBLANES)
        h = a_scr[pl.ds(r0, SUBLANES), :] * hc + u_scr[pl.ds(r0, SUBLANES), :]
        u_scr[pl.ds(r0, SUBLANES), :] = h
        return jnp.broadcast_to(h[SUBLANES - 1:SUBLANES, :], (SUBLANES, LRU_WIDTH))

    hc = lax.fori_loop(0, tile // SUBLANES, body, hc_scr[...])
    hc_scr[...] = hc
    o_ref[...] = (u_scr[...] * _silu(gb_ref[...])).astype(o_ref.dtype)

    @pl.when(t == pl.num_programs(1) - 1)
    def _():
        hlast_ref[...] = hc[0:1, :]


def _lru_weight_specs(layer, nargs):
    def spec(shape):
        idx = (layer,) + (0,) * len(shape)
        if nargs == 1:
            return pl.BlockSpec((None,) + shape, lambda a: idx)
        return pl.BlockSpec((None,) + shape, lambda a, b: idx)
    return [spec((CONV_W, LRU_WIDTH)), spec((1, LRU_WIDTH)),
            spec((LRU_WIDTH, LRU_WIDTH)), spec((1, LRU_WIDTH)),
            spec((LRU_WIDTH, LRU_WIDTH)), spec((1, LRU_WIDTH)), spec((1, LRU_WIDTH))]


def _lru_prompt(z3, lru_w, layer):
    b, s, _ = z3.shape
    tile = min(LRU_TILE, s)
    assert s % tile == 0
    return pl.pallas_call(
        functools.partial(_lru_prompt_kernel, tile=tile),
        grid=(b, s // tile),
        in_specs=[pl.BlockSpec((None, tile, LRU_WIDTH), lambda bi, t: (bi, t, COL_XB // LRU_WIDTH)),
                  pl.BlockSpec((None, tile, LRU_WIDTH), lambda bi, t: (bi, t, COL_GB // LRU_WIDTH))]
        + _lru_weight_specs(layer, 2),
        out_specs=[pl.BlockSpec((None, tile, LRU_WIDTH), lambda bi, t: (bi, t, 0)),
                   pl.BlockSpec((None, 1, LRU_WIDTH), lambda bi, t: (bi, 0, 0))],
        out_shape=[jax.ShapeDtypeStruct((b, s, LRU_WIDTH), BF16),
                   jax.ShapeDtypeStruct((b, 1, LRU_WIDTH), F32)],
        scratch_shapes=[pltpu.VMEM((tile + 2 * SUBLANES, LRU_WIDTH), F32),
                        pltpu.VMEM((tile, LRU_WIDTH), F32),
                        pltpu.VMEM((tile, LRU_WIDTH), F32),
                        pltpu.VMEM((SUBLANES, LRU_WIDTH), F32)],
        compiler_params=_params(("parallel", "arbitrary")),
        name="lru_prompt",
    )(z3, z3, *lru_w)


def _lru_sample_kernel(xb_ref, gb_ref, cs_ref, h0_ref, cw_ref, cb_ref, wa_ref, ba_ref, wx_ref, bx_ref, lam_ref,
                       o_ref, hlast_ref):
    x = xb_ref[...]
    nb, t_len, _ = x.shape
    cs = cs_ref[...]
    row = lax.broadcasted_iota(jnp.int32, (1, t_len, 1), 1)
    cw = cw_ref[...]
    xc = cb_ref[...] + x * cw[CONV_W - 1:CONV_W, :]
    for back in range(1, CONV_W):
        xs = pltpu.roll(x, back, 1)
        for r in range(back):
            src = CONV_W - 1 - back + r
            xs = jnp.where(row == r, cs[:, src:src + 1, :], xs)
        xc = xc + xs * cw[CONV_W - 1 - back:CONV_W - back, :]
    a, u = _lru_gates(xc.reshape(nb * t_len, LRU_WIDTH), wa_ref, ba_ref, wx_ref, bx_ref, lam_ref)
    a = a.reshape(nb, t_len, LRU_WIDTH)
    u = u.reshape(nb, t_len, LRU_WIDTH)
    a, u = _scan8(a, u, 1, row)
    h = a * h0_ref[...] + u
    o_ref[...] = h * _silu(gb_ref[...])
    hlast_ref[...] = h[:, t_len - 1:t_len, :]


def _lru_sample(zs3, state_conv, state_lru4, lru_w, layer):
    b, t_len, _ = zs3.shape
    assert t_len == SUBLANES
    return pl.pallas_call(
        _lru_sample_kernel,
        grid=(1,),
        in_specs=[pl.BlockSpec((b, t_len, LRU_WIDTH), lambda i: (0, 0, COL_XB // LRU_WIDTH)),
                  pl.BlockSpec((b, t_len, LRU_WIDTH), lambda i: (0, 0, COL_GB // LRU_WIDTH)),
                  pl.BlockSpec((None, b, CONV_W - 1, LRU_WIDTH), lambda i: (layer, 0, 0, 0)),
                  pl.BlockSpec((None, b, 1, LRU_WIDTH), lambda i: (layer, 0, 0, 0))]
        + _lru_weight_specs(layer, 1),
        out_specs=[pl.BlockSpec((b, t_len, LRU_WIDTH), lambda i: (0, 0, 0)),
                   pl.BlockSpec((b, 1, LRU_WIDTH), lambda i: (0, 0, 0))],
        out_shape=[jax.ShapeDtypeStruct((b, t_len, LRU_WIDTH), F32),
                   jax.ShapeDtypeStruct((b, 1, LRU_WIDTH), F32)],
        compiler_params=_params(("arbitrary",)),
        name="lru_sample",
    )(zs3, zs3, state_conv, state_lru4, *lru_w)


def _split3(x):
    hi = x.astype(BF16)
    r = x - hi.astype(F32)
    mid = r.astype(BF16)
    lo = (r - mid.astype(F32)).astype(BF16)
    return hi, mid, lo


def _pad_rows(x, n):
    if x.shape[0] == n:
        return x
    return jnp.concatenate([x, jnp.zeros((n - x.shape[0], x.shape[1]), x.dtype)], axis=0)


def _lower_bound(logits, layer):
    e = jnp.exp(logits - jnp.max(logits, axis=0, keepdims=True))
    den = jnp.sum(e, axis=0, keepdims=True)
    num = jnp.zeros_like(den)
    for r in range(1, layer + 1):
        num = num + e[r:r + 1, :]
    return num / den


def _chunk_cumsum(x, chunk):
    n = x.shape[0]
    if chunk == SUBLANES:
        x3 = x.reshape(n // SUBLANES, SUBLANES, x.shape[1])
        sub = lax.broadcasted_iota(jnp.int32, (1, SUBLANES, 1), 1)
        for sh in (1, 2, 4):
            x3 = x3 + jnp.where(sub >= sh, pltpu.roll(x3, sh, 1), 0.0)
        return x3.reshape(n, x.shape[1])
    row = lax.broadcasted_iota(jnp.int32, (n, 1), 0)
    col = lax.broadcasted_iota(jnp.int32, (1, n), 1)
    tri = ((row >= col) & (row // chunk == col // chunk)).astype(BF16)
    out = None
    for part in _split3(x):
        term = jnp.dot(tri, part, preferred_element_type=F32)
        out = term if out is None else out + term
    return out


def _hgrn_tile(qc2, fc2, v, gc, st, lb2, cg, chunk, carry=True):
    n = v.shape[0]
    nc = n // chunk
    nblk = chunk // SUBLANES
    lane = lax.broadcasted_iota(jnp.int32, (1, LANES), 1)
    head_a = lane < C_VDIM
    sub = lax.broadcasted_iota(jnp.int32, (1, SUBLANES, 1), 1)
    blocked = lambda x: x.reshape(n // SUBLANES, SUBLANES, LANES)
    chunked = lambda x: x.reshape(nc, chunk, LANES)
    bnt = (((2,), (2,)), ((0,), (0,)))
    bnn = (((2,), (1,)), ((0,), (0,)))
    nt = (((1,), (1,)), ((), ()))

    q2 = _silu(qc2) * (C_KDIM ** -0.5)
    g2 = lb2 + (1.0 - lb2) * jax.nn.sigmoid(fc2)
    k2 = 1.0 - g2
    b2 = _chunk_cumsum(jnp.log(g2) * LOG2_E, chunk)
    v3 = blocked(v)
    vc = chunked(v).astype(BF16)
    local, q_states, decays, k_ends = [], [], [], []
    for a in range(2):
        hs = slice(a * C_KDIM, (a + 1) * C_KDIM)
        q, k, b = q2[:, hs], k2[:, hs], b2[:, hs]
        q3, k3, g3 = blocked(q), blocked(k), blocked(g2[:, hs])
        o3 = jnp.sum(q3 * k3, axis=-1, keepdims=True) * v3
        decay = g3
        for d in range(1, SUBLANES):
            g_back = pltpu.roll(g3, d, 1)
            w = q3 * (1.0 - g_back) * decay
            a_d = jnp.where(sub >= d, jnp.sum(w, axis=-1, keepdims=True), 0.0)
            o3 = o3 + a_d * pltpu.roll(v3, d, 1)
            decay = decay * g_back
        o = o3.reshape(n, LANES)
        qc_, kc_, bc_ = chunked(q), chunked(k), chunked(b)
        if nblk > 1:
            blocks = [jnp.zeros((nc, SUBLANES, chunk), F32)]
            for i in range(1, nblk):
                lo = i * SUBLANES
                ref = bc_[:, lo - 1:lo, :]
                kt = kc_[:, :lo, :] * jnp.exp2(ref - bc_[:, :lo, :])
                kt = jnp.concatenate([kt, jnp.zeros((nc, chunk - lo, LANES), F32)], axis=1).astype(BF16)
                qt = (qc_[:, lo:lo + SUBLANES, :] * jnp.exp2(bc_[:, lo:lo + SUBLANES, :] - ref)).astype(BF16)
                blocks.append(lax.dot_general(qt, kt, bnt, preferred_element_type=F32))
            attn = jnp.concatenate(blocks, axis=1).astype(BF16)
            o = o + lax.dot_general(attn, vc, bnn, preferred_element_type=F32).reshape(n, LANES)
        local.append(o)
        last = bc_[:, chunk - 1:chunk, :]
        decays.append(jnp.exp2(last))
        k_ends.append(kc_ * jnp.exp2(last - bc_))
        q_states.append((qc_ * jnp.exp2(bc_)).astype(BF16))
    top = lax.broadcasted_iota(jnp.int32, (LANES, 1), 0) < C_VDIM
    pad = lambda x: _pad_rows(x, LANES).astype(BF16)
    inter = [[], []]
    new_states = []
    for c in range(nc):
        st_c = st if carry else st[c]
        st_b = st_c.astype(BF16)
        for a in range(2):
            inter[a].append(lax.dot_general(q_states[a][c], st_b, nt, preferred_element_type=F32))
        v_t = _pad_rows(v[c * chunk:(c + 1) * chunk], LANES).T.astype(BF16)
        upd_a = jnp.dot(v_t, pad(k_ends[0][c]), preferred_element_type=F32)
        upd_b = jnp.dot(v_t, pad(k_ends[1][c]), preferred_element_type=F32)
        st_c = jnp.where(top, st_c * decays[0][c] + upd_a, st_c * decays[1][c] + upd_b)
        if carry:
            st = st_c
        else:
            new_states.append(st_c)
    if not carry:
        st = jnp.stack(new_states)
    cat = lambda xs: xs[0] if len(xs) == 1 else jnp.concatenate(xs, axis=0)
    o = jnp.where(head_a, local[0] + cat(inter[0]), local[1] + cat(inter[1]))
    sq = o * o
    ms_a = jnp.sum(jnp.where(head_a, sq, 0.0), axis=-1, keepdims=True) * (1.0 / C_VDIM)
    ms_b = jnp.sum(jnp.where(head_a, 0.0, sq), axis=-1, keepdims=True) * (1.0 / C_VDIM)
    inv = jnp.where(head_a, lax.rsqrt(ms_a + RMS_EPS), lax.rsqrt(ms_b + RMS_EPS))
    return o * inv * cg * _silu(gc), st


def _hgrn_prompt_kernel(qc_ref, fc_ref, ic_ref, gc_ref, lbl_ref, cg_ref,
                        sq_ref, sk_ref, sv_ref, sg_ref, kt_ref, vt_ref, bias_c_ref, bias_n_ref,
                        o_ref, st_ref, so_ref, st_scr, *, layer, tile, row_steps):
    t = pl.program_id(2)
    step = (pl.program_id(0) * pl.num_programs(1) + pl.program_id(1)) * pl.num_programs(2) + t

    @pl.when(step < row_steps)
    def _():
        for r in range(sq_ref.shape[0]):
            so_ref[r] = _attn_sample_row(sq_ref[r], sk_ref[r], sv_ref[r], sg_ref[r], kt_ref[r], vt_ref[r],
                                         bias_c_ref[...], bias_n_ref[...])

    @pl.when(t == 0)
    def _():
        st_scr[...] = jnp.zeros_like(st_scr)

    lb2 = _lower_bound(lbl_ref[...], layer)
    cg = cg_ref[...]

    def body(c, carry):
        r0 = pl.multiple_of(c * HGRN_GROUP, HGRN_GROUP)
        rows = pl.ds(r0, HGRN_GROUP)
        out, st_new = _hgrn_tile(qc_ref[rows, :], fc_ref[rows, :], ic_ref[rows, :], gc_ref[rows, :],
                                 st_scr[...], lb2, cg, HGRN_CHUNK)
        o_ref[rows, :] = out.astype(o_ref.dtype)
        st_scr[...] = st_new
        return carry

    lax.fori_loop(0, tile // HGRN_GROUP, body, 0)

    @pl.when(t == pl.num_programs(2) - 1)
    def _():
        st_ref[...] = st_scr[...]


def _hgrn_prompt(z3, lb_logits, cnorm3, zs3, cache_kt, cache_vt, bias_s, layer):
    b, s, _ = z3.shape
    bs, t_len, _ = zs3.shape
    w_buf = cache_kt.shape[-1]
    tile = min(HGRN_TILE, s)
    assert s % tile == 0 and tile % HGRN_CHUNK == 0
    kw, vw = 2 * C_KDIM, 2 * C_VDIM
    n_t = s // tile
    rows = min(ATT_SAMPLE_ROWS, bs)
    row_steps = bs // rows
    assert bs % rows == 0 and row_steps <= b * HEAD_PAIRS * n_t
    grp = lambda bi, p, t: jnp.minimum((bi * HEAD_PAIRS + p) * n_t + t, row_steps - 1)
    zcol = lambda base: pl.BlockSpec((rows, t_len, A_WIDTH), lambda bi, p, t: (grp(bi, p, t), 0, base // A_WIDTH))
    cache = pl.BlockSpec((None, rows, A_HEADS, HEAD_DIM, w_buf), lambda bi, p, t: (layer, grp(bi, p, t), 0, 0, 0))
    bias_c, bias_n = bias_s
    return pl.pallas_call(
        functools.partial(_hgrn_prompt_kernel, layer=layer, tile=tile, row_steps=row_steps),
        grid=(b, HEAD_PAIRS, n_t),
        in_specs=[pl.BlockSpec((None, tile, kw), lambda bi, p, t: (bi, t, COL_QC // kw + p)),
                  pl.BlockSpec((None, tile, kw), lambda bi, p, t: (bi, t, COL_FC // kw + p)),
                  pl.BlockSpec((None, tile, vw), lambda bi, p, t: (bi, t, COL_IC // vw + p)),
                  pl.BlockSpec((None, tile, vw), lambda bi, p, t: (bi, t, COL_GC // vw + p)),
                  pl.BlockSpec((DEPTH, kw), lambda bi, p, t: (0, p)),
                  pl.BlockSpec((None, 1, vw), lambda bi, p, t: (layer, 0, p)),
                  zcol(COL_QA), zcol(COL_KA), zcol(COL_VA), zcol(COL_GA), cache, cache,
                  pl.BlockSpec(bias_c.shape, lambda bi, p, t: (0, 0)),
                  pl.BlockSpec(bias_n.shape, lambda bi, p, t: (0, 0))],
        out_specs=[pl.BlockSpec((None, tile, vw), lambda bi, p, t: (bi, t, p)),
                   pl.BlockSpec((None, None, LANES, LANES), lambda bi, p, t: (bi, p, 0, 0)),
                   pl.BlockSpec((rows, t_len, A_WIDTH), lambda bi, p, t: (grp(bi, p, t), 0, 0))],
        out_shape=[jax.ShapeDtypeStruct((b, s, C_VWIDTH), BF16),
                   jax.ShapeDtypeStruct((b, HEAD_PAIRS, LANES, LANES), F32),
                   jax.ShapeDtypeStruct((bs, t_len, A_WIDTH), F32)],
        scratch_shapes=[pltpu.VMEM((LANES, LANES), F32)],
        compiler_params=_params(("arbitrary", "arbitrary", "arbitrary")),
        name="hgrn_prompt",
    )(z3, z3, z3, z3, lb_logits, cnorm3, zs3, zs3, zs3, zs3, cache_kt, cache_vt, bias_c, bias_n)


def _hgrn_sample_kernel(qc_ref, fc_ref, ic_ref, gc_ref, st_in_ref, lbl_ref, cg_ref, o_ref, st_ref, *, layer):
    lb = _lower_bound(lbl_ref[...], layer)
    rows, t_len, _ = qc_ref.shape
    kw, vw = 2 * C_KDIM, 2 * C_VDIM
    flat = lambda ref, sl: ref[:, :, sl].reshape(rows * t_len, sl.stop - sl.start)
    for p in range(HEAD_PAIRS):
        ks, vs = slice(p * kw, (p + 1) * kw), slice(p * vw, (p + 1) * vw)
        out, st_new = _hgrn_tile(flat(qc_ref, ks), flat(fc_ref, ks), flat(ic_ref, vs), flat(gc_ref, vs),
                                 st_in_ref[:, p], lb[:, ks], cg_ref[:, vs], t_len, carry=False)
        o_ref[:, :, vs] = out.reshape(rows, t_len, vw)
        st_ref[:, p] = st_new


def _hgrn_sample(zs3, state_t, lb_logits, cnorm3, layer):
    b, t_len, _ = zs3.shape
    rows = min(HGRN_SAMPLE_ROWS, b)
    assert t_len == SUBLANES and b % rows == 0
    return pl.pallas_call(
        functools.partial(_hgrn_sample_kernel, layer=layer),
        grid=(b // rows,),
        in_specs=[pl.BlockSpec((rows, t_len, C_KWIDTH), lambda bi: (bi, 0, COL_QC // C_KWIDTH)),
                  pl.BlockSpec((rows, t_len, C_KWIDTH), lambda bi: (bi, 0, COL_FC // C_KWIDTH)),
                  pl.BlockSpec((rows, t_len, C_VWIDTH), lambda bi: (bi, 0, COL_IC // C_VWIDTH)),
                  pl.BlockSpec((rows, t_len, C_VWIDTH), lambda bi: (bi, 0, COL_GC // C_VWIDTH)),
                  pl.BlockSpec((None, rows, HEAD_PAIRS, LANES, LANES), lambda bi: (layer, bi, 0, 0, 0)),
                  pl.BlockSpec((DEPTH, C_KWIDTH), lambda bi: (0, 0)),
                  pl.BlockSpec((None, 1, C_VWIDTH), lambda bi: (layer, 0, 0))],
        out_specs=[pl.BlockSpec((rows, t_len, C_VWIDTH), lambda bi: (bi, 0, 0)),
                   pl.BlockSpec((rows, HEAD_PAIRS, LANES, LANES), lambda bi: (bi, 0, 0, 0))],
        out_shape=[jax.ShapeDtypeStruct((b, t_len, C_VWIDTH), F32),
                   jax.ShapeDtypeStruct((b, HEAD_PAIRS, LANES, LANES), F32)],
        compiler_params=_params(("parallel",)),
        name="hgrn_sample",
    )(zs3, zs3, zs3, zs3, state_t, lb_logits, cnorm3)


def _block_diag(w):
    depth, nb, d, _ = w.shape
    eye = jnp.eye(nb, dtype=w.dtype)
    full = jnp.einsum("lhij,hg->lhigj", w, eye)
    return full.reshape(depth, nb * d, nb * d).astype(BF16)


def _state_to_pairs(st):
    lead = st.shape[:-3]
    t = jnp.swapaxes(st, -1, -2)
    return t.reshape(lead + (HEAD_PAIRS, 2 * C_VDIM, C_KDIM))


def _pairs_to_state(sp):
    lead = sp.shape[:-3]
    t = sp.reshape(lead + (C_HEADS, C_VDIM, C_KDIM))
    return jnp.swapaxes(t, -1, -2)


def kernel(x_prompt, x_sample, cache_k_win, cache_v_win, state_conv, state_lru, state_hgrn, norm_g, w_in, conv_w,
           conv_b, lru_w_a, lru_b_a, lru_w_x, lru_b_x, lru_lambda, hgrn_lb_logits, hgrn_norm_g, w_out, final_norm_g):
    bp, sp, _ = x_prompt.shape
    bs, ts, _ = x_sample.shape
    w_buf = cache_k_win.shape[2]
    assert w_buf == MAX_WINDOW and sp <= MAX_WINDOW

    w_in_bf = w_in.astype(BF16)
    w_out_bf = w_out.astype(BF16)
    norm_g3 = norm_g.reshape(DEPTH, 1, D_MODEL)
    final_g2 = final_norm_g.reshape(1, D_MODEL)
    row3 = lambda p: p.reshape(DEPTH, 1, -1)
    lru_w = (conv_w, row3(conv_b), _block_diag(lru_w_a), row3(lru_b_a), _block_diag(lru_w_x), row3(lru_b_x),
             row3(lru_lambda))
    cnorm3 = row3(hgrn_norm_g)
    cache_kt = jnp.transpose(cache_k_win, (0, 1, 3, 4, 2))
    cache_vt = jnp.transpose(cache_v_win, (0, 1, 3, 4, 2))
    state_lru4 = state_lru.reshape(DEPTH, bs, 1, LRU_WIDTH)
    state_t = _state_to_pairs(state_hgrn)
    bias_p = _prompt_bias()
    bias_s = _sample_bias(ts)

    xp = x_prompt.reshape(bp * sp, D_MODEL)
    xs = x_sample.reshape(bs * ts, D_MODEL)
    outs = {k: [] for k in ("ks", "vs", "cp", "cs", "lp", "ls", "hp", "hs")}
    kt_all = jnp.zeros((DEPTH, bp, A_HEADS, HEAD_DIM, sp), F32)
    vt_all = jnp.zeros((DEPTH, bp, A_HEADS, HEAD_DIM, sp), F32)
    for layer in range(DEPTH):
        last = layer == DEPTH - 1
        zs = _inproj(xs, norm_g3, w_in_bf, layer)
        zs3 = zs.reshape(bs, ts, IN_WIDTH)
        z = _inproj(xp, norm_g3, w_in_bf, layer)
        z3 = z.reshape(bp, sp, IN_WIDTH)
        mix_a, kt_all, vt_all = _attn_prompt(z3, bias_p, kt_all, vt_all, layer)
        mix_b, h_last = _lru_prompt(z3, lru_w, layer)
        mix_c, st_p, smix_a = _hgrn_prompt(z3, hgrn_lb_logits, cnorm3, zs3, cache_kt, cache_vt, bias_s, layer)
        xp = _outproj(xp, mix_a.reshape(bp * sp, -1), mix_b.reshape(bp * sp, -1), mix_c.reshape(bp * sp, -1),
                      w_out_bf, final_g2, layer, last)
        outs["cp"].append(z3[:, sp - (CONV_W - 1):, COL_XB:COL_XB + LRU_WIDTH])
        outs["lp"].append(h_last.reshape(bp, LRU_WIDTH))
        outs["hp"].append(_pairs_to_state(st_p))
        smix_b, sh_last = _lru_sample(zs3, state_conv, state_lru4, lru_w, layer)
        smix_c, st_s = _hgrn_sample(zs3, state_t, hgrn_lb_logits, cnorm3, layer)
        xs = _outproj(xs, smix_a.reshape(bs * ts, -1), smix_b.reshape(bs * ts, -1), smix_c.reshape(bs * ts, -1),
                      w_out_bf, final_g2, layer, last)
        outs["ks"].append(zs3[:, :, COL_KA:COL_KA + A_WIDTH].reshape(bs, ts, A_HEADS, HEAD_DIM))
        outs["vs"].append(zs3[:, :, COL_VA:COL_VA + A_WIDTH].reshape(bs, ts, A_HEADS, HEAD_DIM))
        outs["cs"].append(zs3[:, ts - (CONV_W - 1):, COL_XB:COL_XB + LRU_WIDTH])
        outs["ls"].append(sh_last.reshape(bs, LRU_WIDTH))
        outs["hs"].append(_pairs_to_state(st_s))
    st = lambda k: jnp.stack(outs[k])
    k_win_p = jnp.transpose(kt_all, (0, 1, 4, 2, 3))
    v_win_p = jnp.transpose(vt_all, (0, 1, 4, 2, 3))
    return (xp.reshape(bp, sp, D_MODEL), xs.reshape(bs, ts, D_MODEL), k_win_p, v_win_p, st("ks"), st("vs"),
            st("cp"), st("cs"), st("lp"), st("ls"), st("hp"), st("hs"))
```

```python
import functools

import jax
import jax.numpy as jnp
import numpy as np
from jax import lax
from jax.experimental import pallas as pl
from jax.experimental.pallas import tpu as pltpu

F32 = jnp.float32
BF16 = jnp.bfloat16

D_MODEL = 1024
DEPTH = 4
HEAD_DIM = 64
A_HEADS = 6
A_WIDTH = A_HEADS * HEAD_DIM
DILATED = ((128, 1), (512, 4), (2048, 16))
MAX_WINDOW = 2048
LRU_BLOCKS = 6
LRU_BLOCK_DIM = 64
LRU_WIDTH = LRU_BLOCKS * LRU_BLOCK_DIM
CONV_W = 4
LRU_C = 8.0
C_HEADS = 6
C_KDIM = 128
C_VDIM = 64
C_KWIDTH = C_HEADS * C_KDIM
C_VWIDTH = C_HEADS * C_VDIM
MIX_WIDTH = A_WIDTH + LRU_WIDTH + C_VWIDTH
IN_WIDTH = 4 * A_WIDTH + 2 * LRU_WIDTH + 2 * C_KWIDTH + 2 * C_VWIDTH
RMS_EPS = 1e-6
LOG2_E = 1.4426950408889634

LANES = 128
SUBLANES = 8
HEAD_PAIRS = A_HEADS // 2
VMEM_LIMIT = 56 * 1024 * 1024

COL_QA, COL_KA, COL_VA, COL_GA = 0, A_WIDTH, 2 * A_WIDTH, 3 * A_WIDTH
COL_XB = 4 * A_WIDTH
COL_GB = COL_XB + LRU_WIDTH
COL_QC = COL_GB + LRU_WIDTH
COL_FC = COL_QC + C_KWIDTH
COL_IC = COL_FC + C_KWIDTH
COL_GC = COL_IC + C_VWIDTH

ATT_TQ = 512
HGRN_CHUNK = 64
HGRN_GROUP = 256
HGRN_TILE = 512
HGRN_SAMPLE_ROWS = 8
ATT_SAMPLE_ROWS = 2
LRU_TILE = 512
PROJ_TM = 512


def _silu(x):
    return x * jax.nn.sigmoid(x)


def _params(sem, **kw):
    return pltpu.CompilerParams(dimension_semantics=sem, vmem_limit_bytes=VMEM_LIMIT, **kw)


def _inproj_kernel(x_ref, g_ref, w_ref, z_ref, *, n_chunk):
    x = x_ref[...]
    h = (x * lax.rsqrt(jnp.mean(x * x, axis=-1, keepdims=True) + RMS_EPS) * g_ref[...]).astype(BF16)
    for n in range(IN_WIDTH // n_chunk):
        sl = slice(n * n_chunk, (n + 1) * n_chunk)
        z_ref[:, sl] = jnp.dot(h, w_ref[:, sl], preferred_element_type=F32)


def _inproj(x2d, norm_g3, w_in_bf, layer):
    m = x2d.shape[0]
    tm = min(PROJ_TM, m)
    assert m % tm == 0
    return pl.pallas_call(
        functools.partial(_inproj_kernel, n_chunk=512),
        grid=(m // tm,),
        in_specs=[
            pl.BlockSpec((tm, D_MODEL), lambda i: (i, 0)),
            pl.BlockSpec((None, 1, D_MODEL), lambda i: (layer, 0, 0)),
            pl.BlockSpec((None, D_MODEL, IN_WIDTH), lambda i: (layer, 0, 0)),
        ],
        out_specs=pl.BlockSpec((tm, IN_WIDTH), lambda i: (i, 0)),
        out_shape=jax.ShapeDtypeStruct((m, IN_WIDTH), F32),
        compiler_params=_params(("parallel",)),
        name="inproj",
    )(x2d, norm_g3, w_in_bf)


def _outproj_kernel(x_ref, ma_ref, mb_ref, mc_ref, w_ref, g_ref, y_ref, *, final_norm):
    y = x_ref[...]
    y = y + jnp.dot(ma_ref[...].astype(BF16), w_ref[0:A_WIDTH, :], preferred_element_type=F32)
    y = y + jnp.dot(mb_ref[...].astype(BF16), w_ref[A_WIDTH:A_WIDTH + LRU_WIDTH, :], preferred_element_type=F32)
    y = y + jnp.dot(mc_ref[...].astype(BF16), w_ref[A_WIDTH + LRU_WIDTH:, :], preferred_element_type=F32)
    if final_norm:
        y = y * lax.rsqrt(jnp.mean(y * y, axis=-1, keepdims=True) + RMS_EPS) * g_ref[...]
    y_ref[...] = y


def _outproj(x2d, mix_a, mix_b, mix_c, w_out_bf, final_g2, layer, final_norm):
    m = x2d.shape[0]
    tm = min(PROJ_TM, m)
    assert m % tm == 0
    mspec = lambda w: pl.BlockSpec((tm, w), lambda i: (i, 0))
    return pl.pallas_call(
        functools.partial(_outproj_kernel, final_norm=final_norm),
        grid=(m // tm,),
        in_specs=[
            mspec(D_MODEL), mspec(A_WIDTH), mspec(LRU_WIDTH), mspec(C_VWIDTH),
            pl.BlockSpec((None, MIX_WIDTH, D_MODEL), lambda i: (layer, 0, 0)),
            pl.BlockSpec((1, D_MODEL), lambda i: (0, 0)),
        ],
        out_specs=mspec(D_MODEL),
        out_shape=jax.ShapeDtypeStruct((m, D_MODEL), F32),
        compiler_params=_params(("parallel",)),
        name="outproj",
    )(x2d, mix_a, mix_b, mix_c, w_out_bf, final_g2)


def _branch_count(delta):
    delta = np.asarray(delta)
    cnt = np.zeros(delta.shape, np.int32)
    for window, dil in DILATED:
        cnt += ((delta >= 0) & (delta <= window) & (delta % dil == 0)).astype(np.int32)
    return cnt


def _log_count(cnt):
    with np.errstate(divide="ignore"):
        return np.where(cnt > 0, np.log(np.maximum(cnt, 1).astype(np.float64)), -np.inf).astype(np.float32)


ATT_FAR_OFF = (DILATED[1][0] + ATT_TQ - 1) // ATT_TQ + 1


def _prompt_bias():
    qi = np.arange(ATT_TQ)[None, :]
    ki = np.arange(ATT_TQ)[:, None]
    tabs = []
    for off in range(ATT_FAR_OFF + 1):
        b = _log_count(_branch_count(off * ATT_TQ + qi - ki)) * np.float32(LOG2_E)
        tabs.append(np.concatenate([b, b], axis=1))
    return jnp.asarray(np.stack(tabs))


def _attn_seq_kernel(q_ref, k_ref, v_ref, g_ref, bias_ref, kt_in_ref, vt_in_ref, o_ref, kt_ref, vt_ref):
    s_len = k_ref.shape[0]
    lane = lax.broadcasted_iota(jnp.int32, (1, LANES), 1)
    head_a = lane < HEAD_DIM
    nt = (((1,), (1,)), ((), ()))
    tn = (((0,), (0,)), ((), ()))
    for i in range(s_len // ATT_TQ):
        rows = slice(i * ATT_TQ, (i + 1) * ATT_TQ)
        q = q_ref[rows, :] * (HEAD_DIM ** -0.5 * LOG2_E)
        q2 = jnp.concatenate([jnp.where(head_a, q, 0.0), jnp.where(head_a, 0.0, q)], axis=0).astype(BF16)
        n_keys = (i + 1) * ATT_TQ
        kb = k_ref[0:n_keys, :].astype(BF16)
        vb = v_ref[0:n_keys, :].astype(BF16)
        bias = [bias_ref[min(i - c, ATT_FAR_OFF)] for c in range(i + 1)]
        bias = bias[0] if len(bias) == 1 else jnp.concatenate(bias, axis=0)
        s = lax.dot_general(kb, q2, nt, preferred_element_type=F32) + bias
        m = jnp.max(s, axis=0, keepdims=True)
        p = jnp.exp2(s - m)
        l = jnp.sum(p, axis=0, keepdims=True)
        acc = lax.dot_general(vb, p.astype(BF16), tn, preferred_element_type=F32)
        o2 = (acc / l).T
        o = jnp.where(head_a, o2[:ATT_TQ], o2[ATT_TQ:])
        o_ref[rows, :] = (o * _silu(g_ref[rows, :])).astype(o_ref.dtype)
    for src, dst in ((k_ref, kt_ref), (v_ref, vt_ref)):
        for c in range(s_len // LANES):
            blk = src[c * LANES:(c + 1) * LANES, :].T
            dst[:, :, c * LANES:(c + 1) * LANES] = blk.reshape(2, HEAD_DIM, LANES)


def _attn_prompt(z3, bias, kt_all, vt_all, layer):
    b, s, _ = z3.shape
    assert s % ATT_TQ == 0 and s <= MAX_WINDOW and ATT_TQ % DILATED[2][1] == 0
    col = lambda base: pl.BlockSpec((None, s, LANES), lambda bi, p: (bi, 0, base // LANES + p))
    win = pl.BlockSpec((None, None, 2, HEAD_DIM, s), lambda bi, p: (layer, bi, p, 0, 0))
    return pl.pallas_call(
        _attn_seq_kernel,
        grid=(b, HEAD_PAIRS),
        in_specs=[col(COL_QA), col(COL_KA), col(COL_VA), col(COL_GA),
                  pl.BlockSpec((ATT_FAR_OFF + 1, ATT_TQ, 2 * ATT_TQ), lambda bi, p: (0, 0, 0)),
                  pl.BlockSpec(memory_space=pl.ANY),
                  pl.BlockSpec(memory_space=pl.ANY)],
        out_specs=[pl.BlockSpec((None, s, LANES), lambda bi, p: (bi, 0, p)), win, win],
        out_shape=[jax.ShapeDtypeStruct((b, s, A_WIDTH), BF16),
                   jax.ShapeDtypeStruct(kt_all.shape, F32), jax.ShapeDtypeStruct(vt_all.shape, F32)],
        input_output_aliases={5: 1, 6: 2},
        compiler_params=_params(("parallel", "parallel")),
        name="attn_prompt",
    )(z3, z3, z3, z3, bias, kt_all, vt_all)


def _sample_bias(t_len):
    qpos = MAX_WINDOW + np.arange(t_len)[:, None]
    bias_c = _log_count(_branch_count(qpos - np.arange(MAX_WINDOW)[None, :]))
    bias_n = _log_count(_branch_count(qpos - (MAX_WINDOW + np.arange(t_len)[None, :])))
    return jnp.asarray(bias_c), jnp.asarray(bias_n)


def _split_heads(x):
    return jnp.stack([x[:, h * HEAD_DIM:(h + 1) * HEAD_DIM] for h in range(A_HEADS)], axis=0)


def _attn_sample_row(q, k, v, g, kt, vt, bias_c, bias_n):
    q3 = _split_heads(q * (HEAD_DIM ** -0.5))
    kn3 = _split_heads(k)
    vn3 = _split_heads(v)
    bdot = lambda a, b, ca, cb: lax.dot_general(a, b, (((ca,), (cb,)), ((0,), (0,))), preferred_element_type=F32)
    s_c = bdot(q3, kt, 2, 1) + bias_c[None]
    s_n = bdot(q3, kn3, 2, 2) + bias_n[None]
    m = jnp.maximum(jnp.max(s_c, axis=-1, keepdims=True), jnp.max(s_n, axis=-1, keepdims=True))
    p_c = jnp.exp(s_c - m)
    p_n = jnp.exp(s_n - m)
    l = jnp.sum(p_c, axis=-1, keepdims=True) + jnp.sum(p_n, axis=-1, keepdims=True)
    acc = bdot(p_c, vt, 2, 2) + bdot(p_n, vn3, 2, 1)
    o3 = acc / l
    o = jnp.concatenate([o3[h] for h in range(A_HEADS)], axis=-1)
    return o * _silu(g)


def _neg_expm1(x):
    series = -x * (1.0 + x * (0.5 + x * (1.0 / 6.0 + x * (1.0 / 24.0 + x * (1.0 / 120.0)))))
    return jnp.where(x > -0.1, series, 1.0 - jnp.exp(x))


def _softplus(y):
    return jnp.maximum(y, 0.0) + jnp.log1p(jnp.exp(-jnp.abs(y)))


def _lru_gates(xc, wa_ref, ba_ref, wx_ref, bx_ref, lam_ref):
    xcb = xc.astype(BF16)
    r = jax.nn.sigmoid(jnp.dot(xcb, wa_ref[...], preferred_element_type=F32) + ba_ref[...])
    i = jax.nn.sigmoid(jnp.dot(xcb, wx_ref[...], preferred_element_type=F32) + bx_ref[...])
    log_a = -LRU_C * r * _softplus(-lam_ref[...])
    a = jnp.exp(log_a)
    u = jnp.sqrt(_neg_expm1(2.0 * log_a)) * (i * xc)
    return a, u


def _scan8(a, u, axis, row):
    for sh in (1, 2, 4):
        keep = row >= sh
        a_prev = jnp.where(keep, pltpu.roll(a, sh, axis), 1.0)
        u_prev = jnp.where(keep, pltpu.roll(u, sh, axis), 0.0)
        u = a * u_prev + u
        a = a * a_prev
    return a, u


def _lru_prompt_kernel(xb_ref, gb_ref, cw_ref, cb_ref, wa_ref, ba_ref, wx_ref, bx_ref, lam_ref,
                       o_ref, hlast_ref, xp_scr, a_scr, u_scr, hc_scr, *, tile):
    t = pl.program_id(1)

    @pl.when(t == 0)
    def _():
        xp_scr[0:SUBLANES, :] = jnp.zeros((SUBLANES, LRU_WIDTH), F32)
        hc_scr[...] = jnp.zeros_like(hc_scr)

    xp_scr[SUBLANES:SUBLANES + tile, :] = xb_ref[...]
    cw = cw_ref[...]
    xc = cb_ref[...]
    for j in range(CONV_W):
        off = SUBLANES - (CONV_W - 1) + j
        xc = xc + xp_scr[off:off + tile, :] * cw[j:j + 1, :]
    xp_scr[0:SUBLANES, :] = xp_scr[tile:tile + SUBLANES, :]

    a, u = _lru_gates(xc, wa_ref, ba_ref, wx_ref, bx_ref, lam_ref)
    row = lax.broadcasted_iota(jnp.int32, (1, SUBLANES, 1), 1)
    grouped = (tile // SUBLANES, SUBLANES, LRU_WIDTH)
    a, u = _scan8(a.reshape(grouped), u.reshape(grouped), 1, row)
    a_scr[...] = a.reshape(tile, LRU_WIDTH)
    u_scr[...] = u.reshape(tile, LRU_WIDTH)

    def body(gi, hc):
        r0 = pl.multiple_of(gi * SUBLANES, SUBLANES)
        h = a_scr[pl.ds(r0, SUBLANES), :] * hc + u_scr[pl.ds(r0, SUBLANES), :]
        u_scr[pl.ds(r0, SUBLANES), :] = h
        return jnp.broadcast_to(h[SUBLANES - 1:SUBLANES, :], (SUBLANES, LRU_WIDTH))

    hc = lax.fori_loop(0, tile // SUBLANES, body, hc_scr[...])
    hc_scr[...] = hc
    o_ref[...] = (u_scr[...] * _silu(gb_ref[...])).astype(o_ref.dtype)

    @pl.when(t == pl.num_programs(1) - 1)
    def _():
        hlast_ref[...] = hc[0:1, :]


def _lru_weight_specs(layer, nargs):
    def spec(shape):
        idx = (layer,) + (0,) * len(shape)
        if nargs == 1:
            return pl.BlockSpec((None,) + shape, lambda a: idx)
        return pl.BlockSpec((None,) + shape, lambda a, b: idx)
    return [spec((CONV_W, LRU_WIDTH)), spec((1, LRU_WIDTH)),
            spec((LRU_WIDTH, LRU_WIDTH)), spec((1, LRU_WIDTH)),
            spec((LRU_WIDTH, LRU_WIDTH)), spec((1, LRU_WIDTH)), spec((1, LRU_WIDTH))]


def _lru_prompt(z3, lru_w, layer):
    b, s, _ = z3.shape
    tile = min(LRU_TILE, s)
    assert s % tile == 0
    return pl.pallas_call(
        functools.partial(_lru_prompt_kernel, tile=tile),
        grid=(b, s // tile),
        in_specs=[pl.BlockSpec((None, tile, LRU_WIDTH), lambda bi, t: (bi, t, COL_XB // LRU_WIDTH)),
                  pl.BlockSpec((None, tile, LRU_WIDTH), lambda bi, t: (bi, t, COL_GB // LRU_WIDTH))]
        + _lru_weight_specs(layer, 2),
        out_specs=[pl.BlockSpec((None, tile, LRU_WIDTH), lambda bi, t: (bi, t, 0)),
                   pl.BlockSpec((None, 1, LRU_WIDTH), lambda bi, t: (bi, 0, 0))],
        out_shape=[jax.ShapeDtypeStruct((b, s, LRU_WIDTH), BF16),
                   jax.ShapeDtypeStruct((b, 1, LRU_WIDTH), F32)],
        scratch_shapes=[pltpu.VMEM((tile + 2 * SUBLANES, LRU_WIDTH), F32),
                        pltpu.VMEM((tile, LRU_WIDTH), F32),
                        pltpu.VMEM((tile, LRU_WIDTH), F32),
                        pltpu.VMEM((SUBLANES, LRU_WIDTH), F32)],
        compiler_params=_params(("parallel", "arbitrary")),
        name="lru_prompt",
    )(z3, z3, *lru_w)


def _lru_sample_kernel(xb_ref, gb_ref, cs_ref, h0_ref, cw_ref, cb_ref, wa_ref, ba_ref, wx_ref, bx_ref, lam_ref,
                       o_ref, hlast_ref):
    x = xb_ref[...]
    nb, t_len, _ = x.shape
    cs = cs_ref[...]
    row = lax.broadcasted_iota(jnp.int32, (1, t_len, 1), 1)
    cw = cw_ref[...]
    xc = cb_ref[...] + x * cw[CONV_W - 1:CONV_W, :]
    for back in range(1, CONV_W):
        xs = pltpu.roll(x, back, 1)
        for r in range(back):
            src = CONV_W - 1 - back + r
            xs = jnp.where(row == r, cs[:, src:src + 1, :], xs)
        xc = xc + xs * cw[CONV_W - 1 - back:CONV_W - back, :]
    a, u = _lru_gates(xc.reshape(nb * t_len, LRU_WIDTH), wa_ref, ba_ref, wx_ref, bx_ref, lam_ref)
    a = a.reshape(nb, t_len, LRU_WIDTH)
    u = u.reshape(nb, t_len, LRU_WIDTH)
    a, u = _scan8(a, u, 1, row)
    h = a * h0_ref[...] + u
    o_ref[...] = h * _silu(gb_ref[...])
    hlast_ref[...] = h[:, t_len - 1:t_len, :]


def _lru_sample(zs3, state_conv, state_lru4, lru_w, layer):
    b, t_len, _ = zs3.shape
    assert t_len == SUBLANES
    return pl.pallas_call(
        _lru_sample_kernel,
        grid=(1,),
        in_specs=[pl.BlockSpec((b, t_len, LRU_WIDTH), lambda i: (0, 0, COL_XB // LRU_WIDTH)),
                  pl.BlockSpec((b, t_len, LRU_WIDTH), lambda i: (0, 0, COL_GB // LRU_WIDTH)),
                  pl.BlockSpec((None, b, CONV_W - 1, LRU_WIDTH), lambda i: (layer, 0, 0, 0)),
                  pl.BlockSpec((None, b, 1, LRU_WIDTH), lambda i: (layer, 0, 0, 0))]
        + _lru_weight_specs(layer, 1),
        out_specs=[pl.BlockSpec((b, t_len, LRU_WIDTH), lambda i: (0, 0, 0)),
                   pl.BlockSpec((b, 1, LRU_WIDTH), lambda i: (0, 0, 0))],
        out_shape=[jax.ShapeDtypeStruct((b, t_len, LRU_WIDTH), F32),
                   jax.ShapeDtypeStruct((b, 1, LRU_WIDTH), F32)],
        compiler_params=_params(("arbitrary",)),
        name="lru_sample",
    )(zs3, zs3, state_conv, state_lru4, *lru_w)


def _split3(x):
    hi = x.astype(BF16)
    r = x - hi.astype(F32)
    mid = r.astype(BF16)
    lo = (r - mid.astype(F32)).astype(BF16)
    return hi, mid, lo


def _pad_rows(x, n):
    if x.shape[0] == n:
        return x
    return jnp.concatenate([x, jnp.zeros((n - x.shape[0], x.shape[1]), x.dtype)], axis=0)


def _lower_bound(logits, layer):
    e = jnp.exp(logits - jnp.max(logits, axis=0, keepdims=True))
    den = jnp.sum(e, axis=0, keepdims=True)
    num = jnp.zeros_like(den)
    for r in range(1, layer + 1):
        num = num + e[r:r + 1, :]
    return num / den


def _chunk_cumsum(x, chunk):
    n = x.shape[0]
    if chunk == SUBLANES:
        x3 = x.reshape(n // SUBLANES, SUBLANES, x.shape[1])
        sub = lax.broadcasted_iota(jnp.int32, (1, SUBLANES, 1), 1)
        for sh in (1, 2, 4):
            x3 = x3 + jnp.where(sub >= sh, pltpu.roll(x3, sh, 1), 0.0)
        return x3.reshape(n, x.shape[1])
    row = lax.broadcasted_iota(jnp.int32, (n, 1), 0)
    col = lax.broadcasted_iota(jnp.int32, (1, n), 1)
    tri = ((row >= col) & (row // chunk == col // chunk)).astype(BF16)
    out = None
    for part in _split3(x):
        term = jnp.dot(tri, part, preferred_element_type=F32)
        out = term if out is None else out + term
    return out


def _hgrn_tile(qc2, fc2, v, gc, st, lb2, cg, chunk, carry=True):
    n = v.shape[0]
    nc = n // chunk
    nblk = chunk // SUBLANES
    lane = lax.broadcasted_iota(jnp.int32, (1, LANES), 1)
    head_a = lane < C_VDIM
    sub = lax.broadcasted_iota(jnp.int32, (1, SUBLANES, 1), 1)
    blocked = lambda x: x.reshape(n // SUBLANES, SUBLANES, LANES)
    chunked = lambda x: x.reshape(nc, chunk, LANES)
    bnt = (((2,), (2,)), ((0,), (0,)))
    bnn = (((2,), (1,)), ((0,), (0,)))
    nt = (((1,), (1,)), ((), ()))

    q2 = _silu(qc2) * (C_KDIM ** -0.5)
    g2 = lb2 + (1.0 - lb2) * jax.nn.sigmoid(fc2)
    k2 = 1.0 - g2
    b2 = _chunk_cumsum(jnp.log(g2) * LOG2_E, chunk)
    v3 = blocked(v)
    vc = chunked(v).astype(BF16)
    local, q_states, decays, k_ends = [], [], [], []
    for a in range(2):
        hs = slice(a * C_KDIM, (a + 1) * C_KDIM)
        q, k, b = q2[:, hs], k2[:, hs], b2[:, hs]
        q3, k3, g3 = blocked(q), blocked(k), blocked(g2[:, hs])
        o3 = jnp.sum(q3 * k3, axis=-1, keepdims=True) * v3
        decay = g3
        for d in range(1, SUBLANES):
            g_back = pltpu.roll(g3, d, 1)
            w = q3 * (1.0 - g_back) * decay
            a_d = jnp.where(sub >= d, jnp.sum(w, axis=-1, keepdims=True), 0.0)
            o3 = o3 + a_d * pltpu.roll(v3, d, 1)
            decay = decay * g_back
        o = o3.reshape(n, LANES)
        qc_, kc_, bc_ = chunked(q), chunked(k), chunked(b)
        if nblk > 1:
            blocks = [jnp.zeros((nc, SUBLANES, chunk), F32)]
            for i in range(1, nblk):
                lo = i * SUBLANES
                ref = bc_[:, lo - 1:lo, :]
                kt = kc_[:, :lo, :] * jnp.exp2(ref - bc_[:, :lo, :])
                kt = jnp.concatenate([kt, jnp.zeros((nc, chunk - lo, LANES), F32)], axis=1).astype(BF16)
                qt = (qc_[:, lo:lo + SUBLANES, :] * jnp.exp2(bc_[:, lo:lo + SUBLANES, :] - ref)).astype(BF16)
                blocks.append(lax.dot_general(qt, kt, bnt, preferred_element_type=F32))
            attn = jnp.concatenate(blocks, axis=1).astype(BF16)
            o = o + lax.dot_general(attn, vc, bnn, preferred_element_type=F32).reshape(n, LANES)
        local.append(o)
        last = bc_[:, chunk - 1:chunk, :]
        decays.append(jnp.exp2(last))
        k_ends.append(kc_ * jnp.exp2(last - bc_))
        q_states.append((qc_ * jnp.exp2(bc_)).astype(BF16))
    top = lax.broadcasted_iota(jnp.int32, (LANES, 1), 0) < C_VDIM
    pad = lambda x: _pad_rows(x, LANES).astype(BF16)
    inter = [[], []]
    new_states = []
    for c in range(nc):
        st_c = st if carry else st[c]
        st_b = st_c.astype(BF16)
        for a in range(2):
            inter[a].append(lax.dot_general(q_states[a][c], st_b, nt, preferred_element_type=F32))
        v_t = _pad_rows(v[c * chunk:(c + 1) * chunk], LANES).T.astype(BF16)
        upd_a = jnp.dot(v_t, pad(k_ends[0][c]), preferred_element_type=F32)
        upd_b = jnp.dot(v_t, pad(k_ends[1][c]), preferred_element_type=F32)
        st_c = jnp.where(top, st_c * decays[0][c] + upd_a, st_c * decays[1][c] + upd_b)
        if carry:
            st = st_c
        else:
            new_states.append(st_c)
    if not carry:
        st = jnp.stack(new_states)
    cat = lambda xs: xs[0] if len(xs) == 1 else jnp.concatenate(xs, axis=0)
    o = jnp.where(head_a, local[0] + cat(inter[0]), local[1] + cat(inter[1]))
    sq = o * o
    ms_a = jnp.sum(jnp.where(head_a, sq, 0.0), axis=-1, keepdims=True) * (1.0 / C_VDIM)
    ms_b = jnp.sum(jnp.where(head_a, 0.0, sq), axis=-1, keepdims=True) * (1.0 / C_VDIM)
    inv = jnp.where(head_a, lax.rsqrt(ms_a + RMS_EPS), lax.rsqrt(ms_b + RMS_EPS))
    return o * inv * cg * _silu(gc), st


def _hgrn_prompt_kernel(qc_ref, fc_ref, ic_ref, gc_ref, lbl_ref, cg_ref,
                        sq_ref, sk_ref, sv_ref, sg_ref, kt_ref, vt_ref, bias_c_ref, bias_n_ref,
                        o_ref, st_ref, so_ref, st_scr, *, layer, tile, row_steps):
    t = pl.program_id(2)
    step = (pl.program_id(0) * pl.num_programs(1) + pl.program_id(1)) * pl.num_programs(2) + t

    @pl.when(step < row_steps)
    def _():
        for r in range(sq_ref.shape[0]):
            so_ref[r] = _attn_sample_row(sq_ref[r], sk_ref[r], sv_ref[r], sg_ref[r], kt_ref[r], vt_ref[r],
                                         bias_c_ref[...], bias_n_ref[...])

    @pl.when(t == 0)
    def _():
        st_scr[...] = jnp.zeros_like(st_scr)

    lb2 = _lower_bound(lbl_ref[...], layer)
    cg = cg_ref[...]

    def body(c, carry):
        r0 = pl.multiple_of(c * HGRN_GROUP, HGRN_GROUP)
        rows = pl.ds(r0, HGRN_GROUP)
        out, st_new = _hgrn_tile(qc_ref[rows, :], fc_ref[rows, :], ic_ref[rows, :], gc_ref[rows, :],
                                 st_scr[...], lb2, cg, HGRN_CHUNK)
        o_ref[rows, :] = out.astype(o_ref.dtype)
        st_scr[...] = st_new
        return carry

    lax.fori_loop(0, tile // HGRN_GROUP, body, 0)

    @pl.when(t == pl.num_programs(2) - 1)
    def _():
        st_ref[...] = st_scr[...]


def _hgrn_prompt(z3, lb_logits, cnorm3, zs3, cache_kt, cache_vt, bias_s, layer):
    b, s, _ = z3.shape
    bs, t_len, _ = zs3.shape
    w_buf = cache_kt.shape[-1]
    tile = min(HGRN_TILE, s)
    assert s % tile == 0 and tile % HGRN_CHUNK == 0
    kw, vw = 2 * C_KDIM, 2 * C_VDIM
    n_t = s // tile
    rows = min(ATT_SAMPLE_ROWS, bs)
    row_steps = bs // rows
    assert bs % rows == 0 and row_steps <= b * HEAD_PAIRS * n_t
    grp = lambda bi, p, t: jnp.minimum((bi * HEAD_PAIRS + p) * n_t + t, row_steps - 1)
    zcol = lambda base: pl.BlockSpec((rows, t_len, A_WIDTH), lambda bi, p, t: (grp(bi, p, t), 0, base // A_WIDTH))
    cache = pl.BlockSpec((None, rows, A_HEADS, HEAD_DIM, w_buf), lambda bi, p, t: (layer, grp(bi, p, t), 0, 0, 0))
    bias_c, bias_n = bias_s
    return pl.pallas_call(
        functools.partial(_hgrn_prompt_kernel, layer=layer, tile=tile, row_steps=row_steps),
        grid=(b, HEAD_PAIRS, n_t),
        in_specs=[pl.BlockSpec((None, tile, kw), lambda bi, p, t: (bi, t, COL_QC // kw + p)),
                  pl.BlockSpec((None, tile, kw), lambda bi, p, t: (bi, t, COL_FC // kw + p)),
                  pl.BlockSpec((None, tile, vw), lambda bi, p, t: (bi, t, COL_IC // vw + p)),
                  pl.BlockSpec((None, tile, vw), lambda bi, p, t: (bi, t, COL_GC // vw + p)),
                  pl.BlockSpec((DEPTH, kw), lambda bi, p, t: (0, p)),
                  pl.BlockSpec((None, 1, vw), lambda bi, p, t: (layer, 0, p)),
                  zcol(COL_QA), zcol(COL_KA), zcol(COL_VA), zcol(COL_GA), cache, cache,
                  pl.BlockSpec(bias_c.shape, lambda bi, p, t: (0, 0)),
                  pl.BlockSpec(bias_n.shape, lambda bi, p, t: (0, 0))],
        out_specs=[pl.BlockSpec((None, tile, vw), lambda bi, p, t: (bi, t, p)),
                   pl.BlockSpec((None, None, LANES, LANES), lambda bi, p, t: (bi, p, 0, 0)),
                   pl.BlockSpec((rows, t_len, A_WIDTH), lambda bi, p, t: (grp(bi, p, t), 0, 0))],
        out_shape=[jax.ShapeDtypeStruct((b, s, C_VWIDTH), BF16),
                   jax.ShapeDtypeStruct((b, HEAD_PAIRS, LANES, LANES), F32),
                   jax.ShapeDtypeStruct((bs, t_len, A_WIDTH), F32)],
        scratch_shapes=[pltpu.VMEM((LANES, LANES), F32)],
        compiler_params=_params(("arbitrary", "arbitrary", "arbitrary")),
        name="hgrn_prompt",
    )(z3, z3, z3, z3, lb_logits, cnorm3, zs3, zs3, zs3, zs3, cache_kt, cache_vt, bias_c, bias_n)


def _hgrn_sample_kernel(qc_ref, fc_ref, ic_ref, gc_ref, st_in_ref, lbl_ref, cg_ref, st_all_ref, o_ref, st_ref, *,
                        layer):
    lb = _lower_bound(lbl_ref[...], layer)
    rows, t_len, _ = qc_ref.shape
    kw, vw = 2 * C_KDIM, 2 * C_VDIM
    flat = lambda ref, sl: ref[:, :, sl].reshape(rows * t_len, sl.stop - sl.start)
    for p in range(HEAD_PAIRS):
        ks, vs = slice(p * kw, (p + 1) * kw), slice(p * vw, (p + 1) * vw)
        out, st_new = _hgrn_tile(flat(qc_ref, ks), flat(fc_ref, ks), flat(ic_ref, vs), flat(gc_ref, vs),
                                 st_in_ref[:, p], lb[:, ks], cg_ref[:, vs], t_len, carry=False)
        o_ref[:, :, vs] = out.reshape(rows, t_len, vw)
        st_ref[:, p] = st_new


def _hgrn_sample(zs3, state_t, lb_logits, cnorm3, st_all, layer):
    b, t_len, _ = zs3.shape
    rows = min(HGRN_SAMPLE_ROWS, b)
    assert t_len == SUBLANES and b % rows == 0
    return pl.pallas_call(
        functools.partial(_hgrn_sample_kernel, layer=layer),
        grid=(b // rows,),
        in_specs=[pl.BlockSpec((rows, t_len, C_KWIDTH), lambda bi: (bi, 0, COL_QC // C_KWIDTH)),
                  pl.BlockSpec((rows, t_len, C_KWIDTH), lambda bi: (bi, 0, COL_FC // C_KWIDTH)),
                  pl.BlockSpec((rows, t_len, C_VWIDTH), lambda bi: (bi, 0, COL_IC // C_VWIDTH)),
                  pl.BlockSpec((rows, t_len, C_VWIDTH), lambda bi: (bi, 0, COL_GC // C_VWIDTH)),
                  pl.BlockSpec((None, rows, HEAD_PAIRS, LANES, LANES), lambda bi: (layer, bi, 0, 0, 0)),
                  pl.BlockSpec((DEPTH, C_KWIDTH), lambda bi: (0, 0)),
                  pl.BlockSpec((None, 1, C_VWIDTH), lambda bi: (layer, 0, 0)),
                  pl.BlockSpec(memory_space=pl.ANY)],
        out_specs=[pl.BlockSpec((rows, t_len, C_VWIDTH), lambda bi: (bi, 0, 0)),
                   pl.BlockSpec((None, rows, HEAD_PAIRS, LANES, LANES), lambda bi: (layer, bi, 0, 0, 0))],
        out_shape=[jax.ShapeDtypeStruct((b, t_len, C_VWIDTH), F32),
                   jax.ShapeDtypeStruct(st_all.shape, F32)],
        input_output_aliases={7: 1},
        compiler_params=_params(("parallel",)),
        name="hgrn_sample",
    )(zs3, zs3, zs3, zs3, state_t, lb_logits, cnorm3, st_all)


def _block_diag(w):
    depth, nb, d, _ = w.shape
    eye = jnp.eye(nb, dtype=w.dtype)
    full = jnp.einsum("lhij,hg->lhigj", w, eye)
    return full.reshape(depth, nb * d, nb * d).astype(BF16)


def _state_to_pairs(st):
    lead = st.shape[:-3]
    t = jnp.swapaxes(st, -1, -2)
    return t.reshape(lead + (HEAD_PAIRS, 2 * C_VDIM, C_KDIM))


def _pairs_to_state(sp):
    lead = sp.shape[:-3]
    t = sp.reshape(lead + (C_HEADS, C_VDIM, C_KDIM))
    return jnp.swapaxes(t, -1, -2)


def kernel(x_prompt, x_sample, cache_k_win, cache_v_win, state_conv, state_lru, state_hgrn, norm_g, w_in, conv_w,
           conv_b, lru_w_a, lru_b_a, lru_w_x, lru_b_x, lru_lambda, hgrn_lb_logits, hgrn_norm_g, w_out, final_norm_g):
    bp, sp, _ = x_prompt.shape
    bs, ts, _ = x_sample.shape
    w_buf = cache_k_win.shape[2]
    assert w_buf == MAX_WINDOW and sp <= MAX_WINDOW

    w_in_bf = w_in.astype(BF16)
    w_out_bf = w_out.astype(BF16)
    norm_g3 = norm_g.reshape(DEPTH, 1, D_MODEL)
    final_g2 = final_norm_g.reshape(1, D_MODEL)
    row3 = lambda p: p.reshape(DEPTH, 1, -1)
    lru_w = (conv_w, row3(conv_b), _block_diag(lru_w_a), row3(lru_b_a), _block_diag(lru_w_x), row3(lru_b_x),
             row3(lru_lambda))
    cnorm3 = row3(hgrn_norm_g)
    cache_kt = jnp.transpose(cache_k_win, (0, 1, 3, 4, 2))
    cache_vt = jnp.transpose(cache_v_win, (0, 1, 3, 4, 2))
    state_lru4 = state_lru.reshape(DEPTH, bs, 1, LRU_WIDTH)
    state_t = _state_to_pairs(state_hgrn)
    bias_p = _prompt_bias()
    bias_s = _sample_bias(ts)

    xp = x_prompt.reshape(bp * sp, D_MODEL)
    xs = x_sample.reshape(bs * ts, D_MODEL)
    outs = {k: [] for k in ("ks", "vs", "cp", "cs", "lp", "ls", "hp")}
    kt_all = jnp.zeros((DEPTH, bp, A_HEADS, HEAD_DIM, sp), F32)
    vt_all = jnp.zeros((DEPTH, bp, A_HEADS, HEAD_DIM, sp), F32)
    hs_all = jnp.zeros(state_t.shape, F32)
    for layer in range(DEPTH):
        last = layer == DEPTH - 1
        zs = _inproj(xs, norm_g3, w_in_bf, layer)
        zs3 = zs.reshape(bs, ts, IN_WIDTH)
        z = _inproj(xp, norm_g3, w_in_bf, layer)
        z3 = z.reshape(bp, sp, IN_WIDTH)
        mix_a, kt_all, vt_all = _attn_prompt(z3, bias_p, kt_all, vt_all, layer)
        mix_b, h_last = _lru_prompt(z3, lru_w, layer)
        mix_c, st_p, smix_a = _hgrn_prompt(z3, hgrn_lb_logits, cnorm3, zs3, cache_kt, cache_vt, bias_s, layer)
        xp = _outproj(xp, mix_a.reshape(bp * sp, -1), mix_b.reshape(bp * sp, -1), mix_c.reshape(bp * sp, -1),
                      w_out_bf, final_g2, layer, last)
        outs["cp"].append(z3[:, sp - (CONV_W - 1):, COL_XB:COL_XB + LRU_WIDTH])
        outs["lp"].append(h_last.reshape(bp, LRU_WIDTH))
        outs["hp"].append(_pairs_to_state(st_p))
        smix_b, sh_last = _lru_sample(zs3, state_conv, state_lru4, lru_w, layer)
        smix_c, hs_all = _hgrn_sample(zs3, state_t, hgrn_lb_logits, cnorm3, hs_all, layer)
        xs = _outproj(xs, smix_a.reshape(bs * ts, -1), smix_b.reshape(bs * ts, -1), smix_c.reshape(bs * ts, -1),
                      w_out_bf, final_g2, layer, last)
        outs["ks"].append(zs3[:, :, COL_KA:COL_KA + A_WIDTH].reshape(bs, ts, A_HEADS, HEAD_DIM))
        outs["vs"].append(zs3[:, :, COL_VA:COL_VA + A_WIDTH].reshape(bs, ts, A_HEADS, HEAD_DIM))
        outs["cs"].append(zs3[:, ts - (CONV_W - 1):, COL_XB:COL_XB + LRU_WIDTH])
        outs["ls"].append(sh_last.reshape(bs, LRU_WIDTH))
    st = lambda k: jnp.stack(outs[k])
    k_win_p = jnp.transpose(kt_all, (0, 1, 4, 2, 3))
    v_win_p = jnp.transpose(vt_all, (0, 1, 4, 2, 3))
    return (xp.reshape(bp, sp, D_MODEL), xs.reshape(bs, ts, D_MODEL), k_win_p, v_win_p, st("ks"), st("vs"),
            st("cp"), st("cs"), st("lp"), st("ls"), st("hp"), _pairs_to_state(hs_all))
```
